```python
import functools
import jax, jax.numpy as jnp
from jax import lax
import numpy as np

D_MODEL = 1024
BATCH = 8
SEQ = 2048
DEPTH = 2
DEC_BATCH = 32
DEC_SEQ = 8
PAST_LEN = 8192
PAGE_SIZE = 128

HEAD_DIM = 64
HEADS_PER_MIXER = 4
N_MIXERS = 4
N_HEADS = N_MIXERS * HEADS_PER_MIXER
MIX_WIDTH = N_HEADS * HEAD_DIM
BRANCH_WIDTH = HEADS_PER_MIXER * HEAD_DIM
IDX_HEADS = 4
IDX_DIM = 64
DSA_TOPK = 256
MOBA_BLOCK = 256
MOBA_TOPK = 3
Q_BLOCK = 128
ROPE_THETA = 10000.0
D_FF = 4 * D_MODEL
CONV_WIDTH = 3
NORM_EPS = 1e-6
_IN_SIZES = (MIX_WIDTH, MIX_WIDTH, MIX_WIDTH, IDX_HEADS * IDX_DIM, IDX_DIM, IDX_HEADS, HEADS_PER_MIXER, N_MIXERS * D_MODEL)
N_IN = sum(_IN_SIZES)

kernel_name = 'hybrid_dsa_moba_stick_fox_decoder'


def rms_norm(x, g):
    xf = x.astype(jnp.float32)
    y = xf * lax.rsqrt(jnp.mean(xf * xf, axis=-1, keepdims=True) + NORM_EPS)
    return (y * g.astype(jnp.float32)).astype(x.dtype)


def rope(x, pos):
    half = x.shape[-1] // 2
    inv = ROPE_THETA ** (-jnp.arange(half, dtype=jnp.float32) / half)
    ang = pos.astype(jnp.float32)[:, None] * inv[None, :]
    shape = (pos.shape[0],) + (1,) * (x.ndim - 3) + (half,)
    cos = jnp.cos(ang).reshape(shape)
    sin = jnp.sin(ang).reshape(shape)
    xf = x.astype(jnp.float32)
    x1, x2 = xf[..., :half], xf[..., half:]
    return jnp.concatenate([x1 * cos - x2 * sin, x2 * cos + x1 * sin], axis=-1).astype(x.dtype)


def project(h, w_in, g_kidx, b_forget, pos):
    B, T, _ = h.shape
    splits = np.cumsum(_IN_SIZES)[:-1].tolist()
    q, k, v, qi, ki, wi, fl, gl = jnp.split(h @ w_in, splits, axis=-1)
    q = q.reshape(B, T, N_HEADS, HEAD_DIM)
    k = k.reshape(B, T, N_HEADS, HEAD_DIM)
    v = v.reshape(B, T, N_HEADS, HEAD_DIM)
    n_rot = 2 * HEADS_PER_MIXER
    q = jnp.concatenate([rope(q[:, :, :n_rot], pos), q[:, :, n_rot:]], axis=2)
    k = jnp.concatenate([rope(k[:, :, :n_rot], pos), k[:, :, n_rot:]], axis=2)
    qi = rope(qi.reshape(B, T, IDX_HEADS, IDX_DIM), pos)
    ki = rope(rms_norm(ki, g_kidx), pos)
    wi = wi * IDX_HEADS ** -0.5
    logf = jax.nn.log_sigmoid((fl + b_forget).astype(jnp.float32))
    gates = jax.nn.sigmoid(gl.reshape(B, T, N_MIXERS, D_MODEL))
    return q, k, v, qi, ki, wi, logf, gates


def dsa_attend(q, k, v, qi, ki, wi, q_pos, topk):
    S = k.shape[0]
    rel = jax.nn.relu(jnp.einsum('qhe,se->qhs', qi, ki).astype(jnp.float32) * IDX_DIM ** -0.5)
    score = jnp.einsum('qh,qhs->qs', wi.astype(jnp.float32), rel)
    allowed = jnp.arange(S)[None, :] <= q_pos[:, None]
    score = jnp.where(allowed, score, -jnp.inf)
    _, idx = lax.top_k(score, topk)
    valid = idx <= q_pos[:, None]
    ks = k[idx]
    vs = v[idx]
    logits = jnp.einsum('qhd,qkhd->qhk', q, ks).astype(jnp.float32) * HEAD_DIM ** -0.5
    p = jax.nn.softmax(jnp.where(valid[:, None, :], logits, -jnp.inf), axis=-1)
    return jnp.einsum('qhk,qkhd->qhd', p, vs.astype(jnp.float32)).astype(q.dtype)


def moba_attend(q, k, v, q_pos):
    S, H, d = k.shape
    Tq = q.shape[0]
    nblk = -(-S // MOBA_BLOCK)
    pad = nblk * MOBA_BLOCK - S
    kp = jnp.pad(k, ((0, pad), (0, 0), (0, 0))).reshape(nblk, MOBA_BLOCK, H, d)
    vp = jnp.pad(v, ((0, pad), (0, 0), (0, 0))).reshape(nblk, MOBA_BLOCK, H, d)
    cur = q_pos // MOBA_BLOCK
    own = jnp.broadcast_to(cur[:, None, None], (Tq, H, 1))
    kmax = min(MOBA_TOPK, nblk - 1)
    if kmax > 0:
        kbar = jnp.mean(kp.astype(jnp.float32), axis=1)
        sc = jnp.einsum('qhd,nhd->qhn', q.astype(jnp.float32), kbar)
        past = jnp.arange(nblk)[None, :] < cur[:, None]
        sc = jnp.where(past[:, None, :], sc, -jnp.inf)
        top_val, top_idx = lax.top_k(sc, kmax)
        sel = jnp.concatenate([top_idx, own], axis=-1)
        sel_ok = jnp.concatenate([jnp.isfinite(top_val), jnp.ones(own.shape, bool)], axis=-1)
    else:
        sel = own
        sel_ok = jnp.ones(own.shape, bool)
    kb = kp.transpose(2, 0, 1, 3)
    vb = vp.transpose(2, 0, 1, 3)
    hidx = jnp.arange(H)[None, :, None]
    ks = kb[hidx, sel]
    vs = vb[hidx, sel]
    kpos = sel[..., None] * MOBA_BLOCK + jnp.arange(MOBA_BLOCK)
    ok = sel_ok[..., None] & (kpos <= q_pos[:, None, None, None])
    logits = jnp.einsum('qhd,qhjnd->qhjn', q, ks).astype(jnp.float32) * HEAD_DIM ** -0.5
    logits = jnp.where(ok, logits, -jnp.inf)
    J = logits.shape[2]
    p = jax.nn.softmax(logits.reshape(Tq, H, J * MOBA_BLOCK), axis=-1).reshape(Tq, H, J, MOBA_BLOCK)
    return jnp.einsum('qhjn,qhjnd->qhd', p, vs.astype(jnp.float32)).astype(q.dtype)


def stick_breaking(q, k, v, q_pos):
    S = k.shape[0]
    z = jnp.einsum('qhd,shd->hqs', q, k).astype(jnp.float32) * HEAD_DIM ** -0.5
    strict = (jnp.arange(S)[None, :] < q_pos[:, None])[None]
    log_keep = jnp.where(strict, jax.nn.log_sigmoid(-z), 0.0)
    later = lax.cumsum(log_keep, axis=2, reverse=True) - log_keep
    w = jnp.where(strict, jnp.exp(jax.nn.log_sigmoid(z) + later), 0.0)
    return jnp.einsum('hqs,shd->qhd', w, v.astype(jnp.float32)).astype(q.dtype)


def forgetting_attn(q, k, v, cq, ck, q_pos):
    S = k.shape[0]
    logits = jnp.einsum('qhd,shd->hqs', q, k).astype(jnp.float32) * HEAD_DIM ** -0.5
    logits = logits + (cq.T[:, :, None] - ck.T[:, None, :])
    causal = jnp.arange(S)[None, :] <= q_pos[:, None]
    p = jax.nn.softmax(jnp.where(causal[None], logits, -jnp.inf), axis=-1)
    return jnp.einsum('hqs,shd->qhd', p, v.astype(jnp.float32)).astype(q.dtype)


def mix_block(q, qi, wi, cq, q_pos, k, v, ki, ck, topk):
    H = HEADS_PER_MIXER
    oa = dsa_attend(q[:, 0:H], k[:, 0:H], v[:, 0:H], qi, ki, wi, q_pos, topk)
    ob = moba_attend(q[:, H:2 * H], k[:, H:2 * H], v[:, H:2 * H], q_pos)
    oc = stick_breaking(q[:, 2 * H:3 * H], k[:, 2 * H:3 * H], v[:, 2 * H:3 * H], q_pos)
    od = forgetting_attn(q[:, 3 * H:], k[:, 3 * H:], v[:, 3 * H:], cq, ck, q_pos)
    return jnp.concatenate([oa, ob, oc, od], axis=1)


def prompt_mix(q, k, v, qi, ki, wi, logf):
    B, S = q.shape[:2]
    ck = jnp.cumsum(logf, axis=1)
    nqb = S // Q_BLOCK
    topk = min(DSA_TOPK, S // 4)
    k_pos_all = jnp.arange(S)

    def one(i):
        b = i // nqb
        start = (i % nqb) * Q_BLOCK
        sl = lambda a: lax.dynamic_slice_in_dim(a[b], start, Q_BLOCK, axis=0)
        q_pos = start + jnp.arange(Q_BLOCK, dtype=k_pos_all.dtype)
        return mix_block(sl(q), sl(qi), sl(wi), sl(ck), q_pos, k[b], v[b], ki[b], ck[b], topk)

    o = lax.map(one, jnp.arange(B * nqb))
    return o.reshape(B, S, N_HEADS, HEAD_DIM)


def sample_mix(q, k_all, v_all, qi, ki_all, wi, logf_all, past_len):
    ck = jnp.cumsum(logf_all, axis=1)
    L = k_all.shape[1]
    T = q.shape[1]
    topk = min(DSA_TOPK, L // 4)
    q_pos = past_len + jnp.arange(T)
    cq = ck[:, past_len:]
    fn = lambda q_, qi_, wi_, cq_, k_, v_, ki_, ck_: mix_block(q_, qi_, wi_, cq_, q_pos, k_, v_, ki_, ck_, topk)
    return jax.vmap(fn)(q, qi, wi, cq, k_all, v_all, ki_all, ck)


def merge_branches(o, gates, w_branch, w_out):
    B, T = o.shape[:2]
    ob = o.reshape(B, T, N_MIXERS, BRANCH_WIDTH)
    br = jnp.einsum('btmc,mcd->btmd', ob, w_branch)
    return jnp.sum(gates * br, axis=2) @ w_out


def conv_ffn(h, conv_prev, w_ffn_in, conv_w, conv_b, w_ffn_out):
    T = h.shape[1]
    a, b = jnp.split(h @ w_ffn_in, 2, axis=-1)
    a_ext = jnp.concatenate([conv_prev.astype(a.dtype), a], axis=1)
    c = conv_b
    for i in range(CONV_WIDTH):
        c = c + a_ext[:, i:i + T] * conv_w[i]
    out = (jax.nn.gelu(c, approximate=True) * b) @ w_ffn_out
    return out, a_ext[:, -(CONV_WIDTH - 1):]


def setup_inputs(seed: int = 0) -> dict:
    key = jax.random.key(seed)
    ks = jax.random.split(key, 24)
    f32 = jnp.float32
    n_pages = PAST_LEN // PAGE_SIZE
    n_used = DEC_BATCH * n_pages
    n_pool = n_used + n_used // 4
    nrm = lambda k, shape, s: jax.random.normal(k, shape, f32) * s
    x_prompt = nrm(ks[0], (BATCH, SEQ, D_MODEL), 1.0)
    x_sample = nrm(ks[1], (DEC_BATCH, DEC_SEQ, D_MODEL), 1.0)
    cache_k = nrm(ks[2], (DEPTH, n_pool, PAGE_SIZE, N_HEADS, HEAD_DIM), 1.0)
    cache_v = nrm(ks[3], (DEPTH, n_pool, PAGE_SIZE, N_HEADS, HEAD_DIM), 1.0)
    cache_kidx = nrm(ks[4], (DEPTH, n_pool, PAGE_SIZE, IDX_DIM), 1.0)
    cache_logf = jax.nn.log_sigmoid(3.0 + nrm(ks[5], (DEPTH, n_pool, PAGE_SIZE, HEADS_PER_MIXER), 1.0))
    state_conv = nrm(ks[6], (DEPTH, DEC_BATCH, CONV_WIDTH - 1, D_FF), 1.0)
    page_table = jax.random.permutation(ks[7], n_pool)[:n_used].reshape(DEC_BATCH, n_pages).astype(jnp.int32)
    w_in = nrm(ks[8], (DEPTH, D_MODEL, N_IN), D_MODEL ** -0.5)
    g_kidx = 1.0 + nrm(ks[9], (DEPTH, IDX_DIM), 0.01)
    b_forget = 3.0 + nrm(ks[10], (DEPTH, HEADS_PER_MIXER), 0.1)
    w_branch = nrm(ks[11], (DEPTH, N_MIXERS, BRANCH_WIDTH, D_MODEL), BRANCH_WIDTH ** -0.5)
    w_out = nrm(ks[12], (DEPTH, D_MODEL, D_MODEL), D_MODEL ** -0.5)
    w_ffn_in = nrm(ks[13], (DEPTH, D_MODEL, 2 * D_FF), D_MODEL ** -0.5)
    conv_w = nrm(ks[14], (DEPTH, CONV_WIDTH, D_FF), CONV_WIDTH ** -0.5)
    conv_b = nrm(ks[15], (DEPTH, D_FF), 0.01)
    w_ffn_out = nrm(ks[16], (DEPTH, D_FF, D_MODEL), D_FF ** -0.5)
    g_pre_mix = 1.0 + nrm(ks[17], (DEPTH, D_MODEL), 0.01)
    g_post_mix = 1.0 + nrm(ks[18], (DEPTH, D_MODEL), 0.01)
    g_pre_ffn = 1.0 + nrm(ks[19], (DEPTH, D_MODEL), 0.01)
    g_post_ffn = 1.0 + nrm(ks[20], (DEPTH, D_MODEL), 0.01)
    return {'x_prompt': x_prompt, 'x_sample': x_sample, 'cache_k': cache_k, 'cache_v': cache_v,
            'cache_kidx': cache_kidx, 'cache_logf': cache_logf, 'state_conv': state_conv,
            'page_table': page_table, 'w_in': w_in, 'g_kidx': g_kidx, 'b_forget': b_forget,
            'w_branch': w_branch, 'w_out': w_out, 'w_ffn_in': w_ffn_in, 'conv_w': conv_w,
            'conv_b': conv_b, 'w_ffn_out': w_ffn_out, 'g_pre_mix': g_pre_mix, 'g_post_mix': g_post_mix,
            'g_pre_ffn': g_pre_ffn, 'g_post_ffn': g_post_ffn}


def reference(x_prompt, x_sample, cache_k, cache_v, cache_kidx, cache_logf, state_conv, page_table,
              w_in, g_kidx, b_forget, w_branch, w_out, w_ffn_in, conv_w, conv_b, w_ffn_out,
              g_pre_mix, g_post_mix, g_pre_ffn, g_post_ffn):
    n_dec, n_pages = page_table.shape
    past_len = n_pages * cache_k.shape[2]
    pos_p = jnp.arange(x_prompt.shape[1], dtype=jnp.int32)
    pos_s = past_len + jnp.arange(x_sample.shape[1], dtype=jnp.int32)
    conv_zero = jnp.zeros((x_prompt.shape[0], CONV_WIDTH - 1, D_FF), x_prompt.dtype)

    def gather_past(cache, l):
        rows = cache[l, page_table]
        return rows.reshape((n_dec, past_len) + rows.shape[3:])

    yp, ys = x_prompt, x_sample
    kp_l, vp_l, kip_l, lfp_l, cp_l = [], [], [], [], []
    ks_l, vs_l, kis_l, lfs_l, cs_l = [], [], [], [], []
    for l in range(DEPTH):
        q, k, v, qi, ki, wi, logf, gates = project(rms_norm(yp, g_pre_mix[l]), w_in[l], g_kidx[l], b_forget[l], pos_p)
        o = prompt_mix(q, k, v, qi, ki, wi, logf)
        yp = yp + rms_norm(merge_branches(o, gates, w_branch[l], w_out[l]), g_post_mix[l])
        f, conv_new = conv_ffn(rms_norm(yp, g_pre_ffn[l]), conv_zero, w_ffn_in[l], conv_w[l], conv_b[l], w_ffn_out[l])
        yp = yp + rms_norm(f, g_post_ffn[l])
        kp_l.append(k)
        vp_l.append(v)
        kip_l.append(ki)
        lfp_l.append(logf.astype(cache_logf.dtype))
        cp_l.append(conv_new)

        q, k, v, qi, ki, wi, logf, gates = project(rms_norm(ys, g_pre_mix[l]), w_in[l], g_kidx[l], b_forget[l], pos_s)
        k_all = jnp.concatenate([gather_past(cache_k, l).astype(k.dtype), k], axis=1)
        v_all = jnp.concatenate([gather_past(cache_v, l).astype(v.dtype), v], axis=1)
        ki_all = jnp.concatenate([gather_past(cache_kidx, l).astype(ki.dtype), ki], axis=1)
        logf_all = jnp.concatenate([gather_past(cache_logf, l).astype(jnp.float32), logf], axis=1)
        o = sample_mix(q, k_all, v_all, qi, ki_all, wi, logf_all, past_len)
        ys = ys + rms_norm(merge_branches(o, gates, w_branch[l], w_out[l]), g_post_mix[l])
        f, conv_new = conv_ffn(rms_norm(ys, g_pre_ffn[l]), state_conv[l], w_ffn_in[l], conv_w[l], conv_b[l], w_ffn_out[l])
        ys = ys + rms_norm(f, g_post_ffn[l])
        ks_l.append(k)
        vs_l.append(v)
        kis_l.append(ki)
        lfs_l.append(logf.astype(cache_logf.dtype))
        cs_l.append(conv_new)

    return (yp, ys, jnp.stack(kp_l), jnp.stack(vp_l), jnp.stack(kip_l), jnp.stack(lfp_l), jnp.stack(cp_l),
            jnp.stack(ks_l), jnp.stack(vs_l), jnp.stack(kis_l), jnp.stack(lfs_l), jnp.stack(cs_l))
```

```python
import functools

import numpy as np
import jax
import jax.numpy as jnp
from jax import lax
from jax.experimental import pallas as pl
from jax.experimental.pallas import tpu as pltpu

D_MODEL = 1024
HEAD_DIM = 64
N_HEADS = 16
MIX_WIDTH = N_HEADS * HEAD_DIM
BRANCH_WIDTH = 256
N_MIXERS = 4
IDX_HEADS = 4
IDX_DIM = 64
DSA_TOPK = 256
MOBA_BLOCK = 256
MOBA_TOPK = 3
ROPE_THETA = 10000.0
D_FF = 4 * D_MODEL
CONV_WIDTH = 3
NORM_EPS = 1e-6

LANES = 128
SUBLANES = 8
VMEM_LIMIT = 56 * 1024 * 1024

F32 = jnp.float32
BF16 = jnp.bfloat16
NT_DIMS = (((1,), (1,)), ((), ()))
NEG_BIG = -1e30
INT_MIN = -2147483648

PROJ_TM = 256
ATTN_TQ = 128
MERGE_TM = 256
FFN_TM = 512
FFN_TF = 1024


def _rms(x, g):
    return x * lax.rsqrt(jnp.mean(x * x, axis=-1, keepdims=True) + NORM_EPS) * g


def _softplus_neg_abs(x):
    return jnp.log1p(jnp.exp(-jnp.abs(x)))


def _split_dot(a, b_bf16):
    hi = a.astype(BF16)
    lo = (a - hi.astype(F32)).astype(BF16)
    return (jnp.dot(hi, b_bf16, preferred_element_type=F32)
            + jnp.dot(lo, b_bf16, preferred_element_type=F32))


def _split_dot_left(a_bf16, b):
    hi = b.astype(BF16)
    lo = (b - hi.astype(F32)).astype(BF16)
    return (jnp.dot(a_bf16, hi, preferred_element_type=F32)
            + jnp.dot(a_bf16, lo, preferred_element_type=F32))


def _sortable_key(score):
    bits = pltpu.bitcast(score + 0.0, jnp.int32)
    return bits ^ ((bits >> 31) & 0x7FFFFFFF)


def _kth_largest_key(key_ref, kth, rows):
    def body(it, lo):
        cand = lo + lax.shift_left(jnp.int32(1), 31 - it)
        cnt = jnp.sum(jnp.where(key_ref[...] >= cand, 1.0, 0.0), axis=1, keepdims=True)
        return jnp.where(cnt >= kth, cand, lo)
    return lax.fori_loop(0, 32, body, jnp.full((rows, 1), INT_MIN, jnp.int32))


def _topk_select(key_ref, tri_ref, kth, rows, width):
    thr = _kth_largest_key(key_ref, float(kth), rows)
    key = key_ref[...]
    gt = key > thr
    tie = key == thr
    need = float(kth) - jnp.sum(jnp.where(gt, 1.0, 0.0), axis=1, keepdims=True)
    off = jnp.zeros((rows, 1), F32)
    cw = tri_ref.shape[0]
    parts = []
    for c in range(width // cw):
        sl = slice(c * cw, (c + 1) * cw)
        tc = jnp.where(tie[:, sl], 1.0, 0.0).astype(BF16)
        pre = jnp.dot(tc, tri_ref[...], preferred_element_type=F32) + off
        parts.append(jnp.logical_or(gt[:, sl], jnp.logical_and(tie[:, sl], pre <= need)))
        off = pre[:, cw - 1:cw]
    return jnp.concatenate(parts, axis=1)


def _proj_kernel(x_ref, g_ref, wqkv_ref, wqi_ref, wsm_ref, cos_ref, sin_ref, gk_ref, bf_ref, ltri_ref,
                 q_ref, kf_ref, vf_ref, kb_ref, vb_ref, qi_ref, kif_ref, kib_ref, misc_ref, kbar_ref,
                 carry_ref, *, tm, tiles_per_seq):
    i = pl.program_id(0)
    hb = _rms(x_ref[...], g_ref[...]).astype(BF16)
    cos = cos_ref[...]
    sin = sin_ref[...]
    lane = lax.broadcasted_iota(jnp.int32, (tm, LANES), 1)
    first_half = (lane & (HEAD_DIM - 1)) < HEAD_DIM // 2

    def rope(t):
        rot = jnp.where(first_half, pltpu.roll(t, LANES - HEAD_DIM // 2, 1), pltpu.roll(t, HEAD_DIM // 2, 1))
        return t * cos + rot * sin

    n_rot = 4
    for c in range(MIX_WIDTH // LANES):
        sl = slice(c * LANES, (c + 1) * LANES)
        qc = jnp.dot(hb, wqkv_ref[:, sl], preferred_element_type=F32)
        kc = jnp.dot(hb, wqkv_ref[:, MIX_WIDTH + c * LANES:MIX_WIDTH + (c + 1) * LANES],
                     preferred_element_type=F32)
        vc = jnp.dot(hb, wqkv_ref[:, 2 * MIX_WIDTH + c * LANES:2 * MIX_WIDTH + (c + 1) * LANES],
                     preferred_element_type=F32)
        if c < n_rot:
            qc = rope(qc)
            kc = rope(kc)
        q_ref[:, sl] = (qc * HEAD_DIM ** -0.5).astype(q_ref.dtype)
        kf_ref[:, sl] = kc
        kb_ref[:, sl] = kc.astype(BF16)
        vf_ref[:, sl] = vc
        vb_ref[:, sl] = vc.astype(BF16)
        kbar_ref[:, sl] = jnp.mean(kc, axis=0, keepdims=True)
    for c in range(IDX_HEADS * IDX_DIM // LANES):
        sl = slice(c * LANES, (c + 1) * LANES)
        qic = rope(jnp.dot(hb, wqi_ref[:, sl], preferred_element_type=F32))
        qi_ref[:, sl] = (qic * IDX_DIM ** -0.5).astype(qi_ref.dtype)

    sm = jnp.dot(hb, wsm_ref[...], preferred_element_type=F32)
    ki = sm[:, :LANES]
    ki = rope(_rms(ki, gk_ref[...]))
    kif_ref[...] = ki
    kib_ref[...] = ki.astype(BF16)

    mi = sm[:, LANES:]
    lf = mi + bf_ref[...]
    lf = jnp.minimum(lf, 0.0) - _softplus_neg_abs(lf)
    is_cum = jnp.logical_and(lane >= 8, lane < 12)
    cum = _split_dot_left(ltri_ref[...], jnp.where(is_cum, lf, 0.0))
    if tiles_per_seq > 1:
        @pl.when(i % tiles_per_seq == 0)
        def _():
            carry_ref[...] = jnp.zeros_like(carry_ref)
        cum = cum + carry_ref[0:1, :]
        carry_ref[0:1, :] = cum[tm - 1:tm, :]
    misc = jnp.where(lane < 4, mi * IDX_HEADS ** -0.5,
                     jnp.where(lane < 8, lf, jnp.where(lane < 12, cum, 0.0)))
    misc_ref[...] = misc


def _proj_call(x, g, wqkv, wqi, wsm, cos_t, sin_t, gk, bfp, ltri, *, seq_len, q_dtype):
    T = x.shape[0]
    tm = PROJ_TM
    tiles_per_seq = max(seq_len // tm, 1)
    n_pos_tiles = cos_t.shape[0] // tm
    row = lambda w: pl.BlockSpec((tm, w), lambda i: (i, 0))
    const = lambda a: pl.BlockSpec(a.shape, lambda i: (0,) * a.ndim)
    pos = pl.BlockSpec((tm, LANES), lambda i: (i % n_pos_tiles, 0))
    out_shapes = (
        jax.ShapeDtypeStruct((T, MIX_WIDTH), q_dtype),
        jax.ShapeDtypeStruct((T, MIX_WIDTH), F32),
        jax.ShapeDtypeStruct((T, MIX_WIDTH), F32),
        jax.ShapeDtypeStruct((T, MIX_WIDTH), BF16),
        jax.ShapeDtypeStruct((T, MIX_WIDTH), BF16),
        jax.ShapeDtypeStruct((T, IDX_HEADS * IDX_DIM), q_dtype),
        jax.ShapeDtypeStruct((T, LANES), F32),
        jax.ShapeDtypeStruct((T, LANES), BF16),
        jax.ShapeDtypeStruct((T, LANES), F32),
        jax.ShapeDtypeStruct((T // tm, 1, MIX_WIDTH), F32),
    )
    out_specs = (row(MIX_WIDTH), row(MIX_WIDTH), row(MIX_WIDTH), row(MIX_WIDTH), row(MIX_WIDTH),
                 row(IDX_HEADS * IDX_DIM), row(LANES), row(LANES), row(LANES),
                 pl.BlockSpec((None, 1, MIX_WIDTH), lambda i: (i, 0, 0)))
    return pl.pallas_call(
        functools.partial(_proj_kernel, tm=tm, tiles_per_seq=tiles_per_seq),
        grid=(T // tm,),
        in_specs=[row(D_MODEL), const(g), const(wqkv), const(wqi), const(wsm), pos, pos,
                  const(gk), const(bfp), const(ltri)],
        out_specs=out_specs,
        out_shape=out_shapes,
        scratch_shapes=[pltpu.VMEM((SUBLANES, LANES), F32)],
        compiler_params=pltpu.CompilerParams(dimension_semantics=("arbitrary",),
                                             vmem_limit_bytes=VMEM_LIMIT),
        name="proj",
    )(x, g, wqkv, wqi, wsm, cos_t, sin_t, gk, bfp, ltri)


def _half_masked(ref, pair, odd, rows):
    lane = lax.broadcasted_iota(jnp.int32, (rows, LANES), 1)
    keep = (lane >= HEAD_DIM) if odd else (lane < HEAD_DIM)
    blk = ref[:, pair * LANES:(pair + 1) * LANES]
    return jnp.where(keep, blk, jnp.zeros_like(blk))


def _pattn_kernel(q_ref, qi_ref, misc_ref, k_ref, v_ref, ki_ref, ckt_ref, kbar_ref, tri_ref, ust_ref,
                  o_ref, key_ref, *, tq, S):
    i = pl.program_id(1)
    qpos = i * tq + lax.broadcasted_iota(jnp.int32, (tq, 1), 0)
    col = lax.broadcasted_iota(jnp.int32, (1, S), 1)
    causal = col <= qpos
    strict = col < qpos
    misc = misc_ref[...]
    lane = lax.broadcasted_iota(jnp.int32, (tq, LANES), 1)
    lo_half = lane < HEAD_DIM
    nblk = S // MOBA_BLOCK

    def logits_of(h):
        qm = _half_masked(q_ref, h // 2, h % 2, tq)
        return lax.dot_general(qm, k_ref[:, (h // 2) * LANES:(h // 2 + 1) * LANES], NT_DIMS,
                               preferred_element_type=F32)

    def pv(w, h):
        return jnp.dot(w.astype(BF16), v_ref[:, (h // 2) * LANES:(h // 2 + 1) * LANES],
                       preferred_element_type=F32)

    def softmax_pv(lg, h):
        m = jnp.max(lg, axis=1, keepdims=True)
        p = jnp.exp(lg - m)
        l = jnp.sum(p, axis=1, keepdims=True)
        return pv(p, h) * (1.0 / l)

    outs = [None] * N_HEADS

    score = jnp.zeros((tq, S), F32)
    for hi in range(IDX_HEADS):
        qm = _half_masked(qi_ref, hi // 2, hi % 2, tq)
        rel = jnp.maximum(lax.dot_general(qm, ki_ref[...], NT_DIMS, preferred_element_type=F32), 0.0)
        score = score + misc[:, hi:hi + 1] * rel
    key_ref[...] = jnp.where(causal, _sortable_key(score), INT_MIN)
    sel = jnp.logical_and(_topk_select(key_ref, tri_ref, DSA_TOPK, tq, S), causal)
    for h in range(0, 4):
        outs[h] = softmax_pv(jnp.where(sel, logits_of(h), NEG_BIG), h)

    cur = (i * tq) // MOBA_BLOCK
    for h in range(4, 8):
        qm = _half_masked(q_ref, h // 2, h % 2, tq)
        sc = lax.dot_general(qm, kbar_ref[:, (h // 2) * LANES:(h // 2 + 1) * LANES], NT_DIMS,
                             preferred_element_type=F32)
        beaten = jnp.zeros((tq, LANES), F32)
        for m_ in range(nblk):
            bm = sc[:, m_:m_ + 1]
            wins = jnp.logical_or(bm > sc, jnp.logical_and(bm == sc, m_ < lane))
            beaten = beaten + jnp.where(jnp.logical_and(wins, m_ < cur), 1.0, 0.0)
        chosen = jnp.logical_or(jnp.logical_and(lane < cur, beaten < float(MOBA_TOPK)), lane == cur)
        chosen = jnp.where(chosen, 1.0, 0.0)
        lg = logits_of(h)
        parts = []
        for n in range(nblk):
            sl = slice(n * MOBA_BLOCK, (n + 1) * MOBA_BLOCK)
            ok = jnp.logical_and(chosen[:, n:n + 1] > 0.5, causal[:, sl])
            parts.append(jnp.where(ok, lg[:, sl], NEG_BIG))
        outs[h] = softmax_pv(jnp.concatenate(parts, axis=1), h)

    cw = ust_ref.shape[0]
    for h in range(8, 12):
        z = logits_of(h)
        e = _softplus_neg_abs(z)
        log_keep = jnp.where(strict, jnp.minimum(-z, 0.0) - e, 0.0)
        log_hit = jnp.minimum(z, 0.0) - e
        carry = jnp.zeros((tq, 1), F32)
        parts = [None] * (S // cw)
        for c in reversed(range(S // cw)):
            sl = slice(c * cw, (c + 1) * cw)
            later = _split_dot(log_keep[:, sl], ust_ref[...]) + carry
            carry = carry + jnp.sum(log_keep[:, sl], axis=1, keepdims=True)
            parts[c] = jnp.where(strict[:, sl], jnp.exp(log_hit[:, sl] + later), 0.0)
        outs[h] = pv(jnp.concatenate(parts, axis=1), h)

    for h in range(12, 16):
        j = h - 12
        lg = logits_of(h) + (misc[:, 8 + j:9 + j] - ckt_ref[j:j + 1, :])
        outs[h] = softmax_pv(jnp.where(causal, lg, NEG_BIG), h)

    for p_ in range(N_HEADS // 2):
        o_ref[:, p_ * LANES:(p_ + 1) * LANES] = jnp.where(lo_half, outs[2 * p_], outs[2 * p_ + 1]).astype(o_ref.dtype)


def _pattn_call(q, qi, misc, kb, vb, kib, ckt, kbar, tri, ust, *, B, S):
    tq = ATTN_TQ
    nq = S // tq
    T = B * S
    qrow = lambda w: pl.BlockSpec((tq, w), lambda b, i: (b * nq + i, 0))
    seq = lambda w: pl.BlockSpec((S, w), lambda b, i: (b, 0))
    const = lambda a: pl.BlockSpec(a.shape, lambda b, i: (0,) * a.ndim)
    return pl.pallas_call(
        functools.partial(_pattn_kernel, tq=tq, S=S),
        grid=(B, nq),
        in_specs=[qrow(MIX_WIDTH), qrow(IDX_HEADS * IDX_DIM), qrow(LANES), seq(MIX_WIDTH), seq(MIX_WIDTH),
                  seq(LANES), pl.BlockSpec((None, SUBLANES, S), lambda b, i: (b, 0, 0)),
                  pl.BlockSpec((None, LANES, MIX_WIDTH), lambda b, i: (b, 0, 0)), const(tri), const(ust)],
        out_specs=qrow(MIX_WIDTH),
        out_shape=jax.ShapeDtypeStruct((T, MIX_WIDTH), BF16),
        scratch_shapes=[pltpu.VMEM((tq, S), jnp.int32)],
        compiler_params=pltpu.CompilerParams(dimension_semantics=("arbitrary", "arbitrary"),
                                             vmem_limit_bytes=VMEM_LIMIT),
        name="prompt_attn",
    )(q, qi, misc, kb, vb, kib, ckt, kbar, tri, ust)


def _merge_kernel(x_ref, o_ref, gpre_ref, wg_ref, wbr_ref, wout_ref, gpost_ref, y_ref):
    x = x_ref[...]
    hb = _rms(x, gpre_ref[...]).astype(BF16)
    mix = None
    for m in range(N_MIXERS):
        gl = jnp.dot(hb, wg_ref[:, m * D_MODEL:(m + 1) * D_MODEL], preferred_element_type=F32)
        br = jnp.dot(o_ref[:, m * BRANCH_WIDTH:(m + 1) * BRANCH_WIDTH].astype(BF16), wbr_ref[m],
                     preferred_element_type=F32)
        term = jax.nn.sigmoid(gl) * br
        mix = term if mix is None else mix + term
    out = jnp.dot(mix.astype(BF16), wout_ref[...], preferred_element_type=F32)
    y_ref[...] = x + _rms(out, gpost_ref[...])


def _merge_call(x, o, gpre, wg, wbr, wout, gpost):
    T = x.shape[0]
    tm = MERGE_TM
    row = lambda w: pl.BlockSpec((tm, w), lambda i: (i, 0))
    const = lambda a: pl.BlockSpec(a.shape, lambda i: (0,) * a.ndim)
    return pl.pallas_call(
        _merge_kernel,
        grid=(T // tm,),
        in_specs=[row(D_MODEL), row(MIX_WIDTH), const(gpre), const(wg), const(wbr), const(wout), const(gpost)],
        out_specs=row(D_MODEL),
        out_shape=jax.ShapeDtypeStruct((T, D_MODEL), F32),
        compiler_params=pltpu.CompilerParams(dimension_semantics=("arbitrary",),
                                             vmem_limit_bytes=VMEM_LIMIT),
        name="merge",
    )(x, o, gpre, wg, wbr, wout, gpost)


def _ffn_kernel(*refs, tm, tiles_per_seq, has_state):
    if has_state:
        (x_ref, gpre_ref, wa_ref, wb_ref, wo_ref, cw_ref, cb_ref, gpost_ref, e1_ref, e2_ref,
         y_ref, a_ref, hb_ref, acc_ref) = refs
    else:
        (x_ref, gpre_ref, wa_ref, wb_ref, wo_ref, cw_ref, cb_ref, gpost_ref,
         y_ref, a_ref, hb_ref, acc_ref, prev_ref) = refs
    i = pl.program_id(0)
    f = pl.program_id(1)
    nf = pl.num_programs(1)

    @pl.when(f == 0)
    def _():
        hb_ref[...] = _rms(x_ref[...], gpre_ref[...]).astype(BF16)
        acc_ref[...] = jnp.zeros_like(acc_ref)

    hb = hb_ref[...]
    a = jnp.dot(hb, wa_ref[...], preferred_element_type=F32)
    b = jnp.dot(hb, wb_ref[...], preferred_element_type=F32)
    rowi = lax.broadcasted_iota(jnp.int32, a.shape, 0)
    r1 = pltpu.roll(a, 1, 0)
    r2 = pltpu.roll(a, 2, 0)
    if has_state:
        t = rowi & (SUBLANES - 1)
        p1 = jnp.where(t >= 1, r1, e1_ref[...])
        p2 = jnp.where(t >= 2, r2, e2_ref[...])
        a_ref[...] = a
    else:
        @pl.when(i % tiles_per_seq == 0)
        def _():
            prev_ref[f] = jnp.zeros(prev_ref.shape[1:], F32)
        prev = prev_ref[f]
        p1 = jnp.where(rowi >= 1, r1, prev[SUBLANES - 1:SUBLANES, :])
        p2 = jnp.where(rowi >= 2, r2, jnp.where(rowi == 0, prev[SUBLANES - 2:SUBLANES - 1, :],
                                                 prev[SUBLANES - 1:SUBLANES, :]))
        last = a[tm - SUBLANES:, :]
        prev_ref[f] = last
        a_ref[...] = last
    c = cb_ref[...] + p2 * cw_ref[0:1, :] + p1 * cw_ref[1:2, :] + a * cw_ref[2:3, :]
    gated = (jax.nn.gelu(c, approximate=True) * b).astype(BF16)
    acc_ref[...] += jnp.dot(gated, wo_ref[...], preferred_element_type=F32)

    @pl.when(f == nf - 1)
    def _():
        y_ref[...] = x_ref[...] + _rms(acc_ref[...], gpost_ref[...])


def _ffn_call(x, gpre, wa, wb, wo, cw, cb, gpost, *, seq_len, state=None):
    T = x.shape[0]
    tm = min(FFN_TM, T)
    tf = FFN_TF
    nf = D_FF // tf
    has_state = state is not None
    tiles_per_seq = max(seq_len // tm, 1)
    n_seq = T // seq_len
    in_specs = [pl.BlockSpec((tm, D_MODEL), lambda i, f: (i, 0)),
                pl.BlockSpec((1, D_MODEL), lambda i, f: (0, 0)),
                pl.BlockSpec((D_MODEL, tf), lambda i, f: (0, f)),
                pl.BlockSpec((D_MODEL, tf), lambda i, f: (0, f)),
                pl.BlockSpec((tf, D_MODEL), lambda i, f: (f, 0)),
                pl.BlockSpec((SUBLANES, tf), lambda i, f: (0, f)),
                pl.BlockSpec((1, tf), lambda i, f: (0, f)),
                pl.BlockSpec((1, D_MODEL), lambda i, f: (0, 0))]
    args = [x, gpre, wa, wb, wo, cw, cb, gpost]
    scratch = [pltpu.VMEM((tm, D_MODEL), BF16), pltpu.VMEM((tm, D_MODEL), F32)]
    if has_state:
        in_specs += [pl.BlockSpec((tm, tf), lambda i, f: (i, f))] * 2
        args += list(state)
        a_shape = jax.ShapeDtypeStruct((T, D_FF), F32)
        a_spec = pl.BlockSpec((tm, tf), lambda i, f: (i, f))
    else:
        a_shape = jax.ShapeDtypeStruct((T // tm, SUBLANES, D_FF), F32)
        a_spec = pl.BlockSpec((None, SUBLANES, tf), lambda i, f: (i, 0, f))
        scratch.append(pltpu.VMEM((nf, SUBLANES, tf), F32))
    return pl.pallas_call(
        functools.partial(_ffn_kernel, tm=tm, tiles_per_seq=tiles_per_seq, has_state=has_state),
        grid=(T // tm, nf),
        in_specs=in_specs,
        out_specs=(pl.BlockSpec((tm, D_MODEL), lambda i, f: (i, 0)), a_spec),
        out_shape=(jax.ShapeDtypeStruct((T, D_MODEL), F32), a_shape),
        scratch_shapes=scratch,
        compiler_params=pltpu.CompilerParams(dimension_semantics=("arbitrary", "arbitrary"),
                                             vmem_limit_bytes=VMEM_LIMIT),
        name="ffn",
    )(*args)


def _sidx_kernel(pt_ref, qi_ref, misc_ref, kin_ref, tri_ref, hp_ref, ls_ref, kidx_hbm, logf_hbm,
                 sel_ref, ckp_ref, cqb_ref, ckn_ref, kbuf, lbuf, key_ref, sem, *, n_pages, page, lp, pool_base):
    b = pl.program_id(0)
    past = n_pages * page
    tq = SUBLANES

    def copies(p):
        pg = pt_ref[b, p] + pool_base
        return (pltpu.make_async_copy(kidx_hbm.at[pg], kbuf.at[p], sem.at[0]),
                pltpu.make_async_copy(logf_hbm.at[pl.ds(pg, 1)], lbuf.at[pl.ds(p, 1)], sem.at[1]))

    for p in range(n_pages):
        for cp in copies(p):
            cp.start()
    for p in range(n_pages):
        for cp in copies(p):
            cp.wait()

    qi = qi_ref[...]
    qall = jnp.concatenate([qi[:, h * IDX_DIM:(h + 1) * IDX_DIM] for h in range(IDX_HEADS)], axis=0).astype(BF16)
    kpast = kbuf[...].reshape(past, IDX_DIM).astype(BF16)
    rel = jnp.maximum(lax.dot_general(qall, kpast, NT_DIMS, preferred_element_type=F32), 0.0)
    knew = jnp.concatenate([kin_ref[:, :IDX_DIM], jnp.zeros((LANES - tq, IDX_DIM), F32)], axis=0).astype(BF16)
    reln = jnp.maximum(lax.dot_general(qall, knew, NT_DIMS, preferred_element_type=F32), 0.0)
    misc = misc_ref[...]
    sp = jnp.zeros((tq, past), F32)
    sn = jnp.zeros((tq, LANES), F32)
    for h in range(IDX_HEADS):
        sp = sp + misc[:, h:h + 1] * rel[h * tq:(h + 1) * tq, :]
        sn = sn + misc[:, h:h + 1] * reln[h * tq:(h + 1) * tq, :]
    qrow = lax.broadcasted_iota(jnp.int32, (tq, LANES), 0)
    ncol = lax.broadcasted_iota(jnp.int32, (tq, LANES), 1)
    key_ref[:, :past] = _sortable_key(sp)
    key_ref[:, past:past + LANES] = jnp.where(ncol <= qrow, _sortable_key(sn), INT_MIN)
    if lp > past + LANES:
        key_ref[:, past + LANES:] = jnp.full((tq, lp - past - LANES), INT_MIN, jnp.int32)
    sel = _topk_select(key_ref, tri_ref, DSA_TOPK, tq, lp)
    col = lax.broadcasted_iota(jnp.int32, (tq, lp), 1)
    qr = lax.broadcasted_iota(jnp.int32, (tq, lp), 0)
    sel_ref[...] = jnp.where(jnp.logical_and(sel, col <= past + qr), 1.0, 0.0)

    lg = lbuf[...]
    within = _split_dot(lg, hp_ref[...])
    totals = jnp.concatenate(
        [jnp.broadcast_to(within[:, (h + 1) * page - 1:(h + 1) * page], (n_pages, page)) for h in range(4)], axis=1)
    offs = _split_dot_left(ls_ref[...], totals)
    ckp = within + offs
    ckp_ref[...] = ckp
    tot = ckp[n_pages - 1:n_pages, :]
    cq_rows = []
    ckn_rows = []
    mt = jnp.concatenate([misc, jnp.zeros((LANES - tq, LANES), F32)], axis=0).T
    for h in range(4):
        tot_h = tot[:, (h + 1) * page - 1:(h + 1) * page]
        cq_rows.append(jnp.broadcast_to(misc[:, 8 + h:9 + h] + tot_h, (tq, LANES)))
        ckn_rows.append(mt[8 + h:9 + h, :] + tot_h)
    cqb_ref[...] = jnp.concatenate(cq_rows, axis=0)
    ckn_ref[...] = jnp.concatenate(ckn_rows + [jnp.zeros((SUBLANES - 4, LANES), F32)], axis=0)


def _sidx_call(page_table, qi, misc, kif, tri, hp, ls, kidx_pool, logf_pool, *, lp, pool_base):
    n_dec, n_pages = page_table.shape
    page = kidx_pool.shape[1]
    tq = SUBLANES
    blk = lambda w: pl.BlockSpec((tq, w), lambda b, pt: (b, 0))
    const = lambda a: pl.BlockSpec(a.shape, lambda b, pt: (0,) * a.ndim)
    grid_spec = pltpu.PrefetchScalarGridSpec(
        num_scalar_prefetch=1,
        grid=(n_dec,),
        in_specs=[blk(IDX_HEADS * IDX_DIM), blk(LANES), blk(LANES), const(tri), const(hp), const(ls),
                  pl.BlockSpec(memory_space=pl.ANY), pl.BlockSpec(memory_space=pl.ANY)],
        out_specs=(pl.BlockSpec((None, tq, lp), lambda b, pt: (b, 0, 0)),
                   pl.BlockSpec((None, n_pages, 4 * page), lambda b, pt: (b, 0, 0)),
                   pl.BlockSpec((None, 4 * tq, LANES), lambda b, pt: (b, 0, 0)),
                   pl.BlockSpec((None, SUBLANES, LANES), lambda b, pt: (b, 0, 0))),
        scratch_shapes=[pltpu.VMEM((n_pages, page, IDX_DIM), F32),
                        pltpu.VMEM((n_pages, 4 * page), F32),
                        pltpu.VMEM((tq, lp), jnp.int32),
                        pltpu.SemaphoreType.DMA((2,))],
    )
    return pl.pallas_call(
        functools.partial(_sidx_kernel, n_pages=n_pages, page=page, lp=lp, pool_base=pool_base),
        grid_spec=grid_spec,
        out_shape=(jax.ShapeDtypeStruct((n_dec, tq, lp), F32),
                   jax.ShapeDtypeStruct((n_dec, n_pages, 4 * page), F32),
                   jax.ShapeDtypeStruct((n_dec, 4 * tq, LANES), F32),
                   jax.ShapeDtypeStruct((n_dec, SUBLANES, LANES), F32)),
        compiler_params=pltpu.CompilerParams(dimension_semantics=("arbitrary",),
                                             vmem_limit_bytes=VMEM_LIMIT),
        name="sample_index",
    )(page_table, qi, misc, kif, tri, hp, ls, kidx_pool, logf_pool)


def _sattn_kernel(pt_ref, q_ref, kn_ref, vn_ref, selp_ref, seln_ref, ckp_ref, cqb_ref, ckn_ref, ust_ref,
                  ka_ref, kb_ref, va_ref, vb_ref, o_ref,
                  qs_ref, ml_ref, acc_ref, carry_ref, bst_ref, bacc_ref, *, nblk, page):
    j = pl.program_id(1)
    R = 4 * SUBLANES
    W = BRANCH_WIDTH
    rowi = lax.broadcasted_iota(jnp.int32, (R, W), 0)
    lanei = lax.broadcasted_iota(jnp.int32, (R, W), 1)
    own = (rowi // SUBLANES) == (lanei // HEAD_DIM)

    def online(m_idx, lg, vmix):
        m_old = ml_ref[m_idx, :, 0:1]
        l_old = ml_ref[m_idx, :, 1:2]
        m_new = jnp.maximum(m_old, jnp.max(lg, axis=1, keepdims=True))
        alpha = jnp.exp(m_old - m_new)
        p = jnp.exp(lg - m_new)
        ml_ref[m_idx, :, 0:1] = m_new
        ml_ref[m_idx, :, 1:2] = alpha * l_old + jnp.sum(p, axis=1, keepdims=True)
        acc_ref[m_idx] = alpha * acc_ref[m_idx] + jnp.dot(p.astype(BF16), vmix, preferred_element_type=F32)

    def block_partial(n, lg, vmix, kmean):
        m_n = jnp.max(lg, axis=1, keepdims=True)
        p = jnp.exp(lg - m_n)
        l_n = jnp.sum(p, axis=1, keepdims=True)
        sc_n = jnp.sum(qs_ref[1].astype(F32) * kmean, axis=1, keepdims=True)
        li = lax.broadcasted_iota(jnp.int32, (R, LANES), 1)
        bst_ref[n] = jnp.where(li == 0, m_n, jnp.where(li == 1, l_n, sc_n))
        bacc_ref[n] = jnp.dot(p.astype(BF16), vmix, preferred_element_type=F32)

    def stick(z, strict, vmix, ust):
        e = _softplus_neg_abs(z)
        log_keep = jnp.minimum(-z, 0.0) - e
        log_hit = jnp.minimum(z, 0.0) - e
        if strict is not None:
            log_keep = jnp.where(strict, log_keep, 0.0)
        carry = carry_ref[:, 0:1]
        later = _split_dot(log_keep, ust) + carry
        carry_ref[:, 0:1] = carry + jnp.sum(log_keep, axis=1, keepdims=True)
        w = jnp.exp(log_hit + later)
        if strict is not None:
            w = jnp.where(strict, w, 0.0)
        acc_ref[2] = acc_ref[2] + jnp.dot(w.astype(BF16), vmix, preferred_element_type=F32)

    def rows4(x8):
        return jnp.concatenate([x8] * 4, axis=0)

    @pl.when(j == 0)
    def _():
        q = q_ref[...]
        for m in range(N_MIXERS):
            qm = rows4(q[:, m * W:(m + 1) * W])
            qs_ref[m] = jnp.where(own, qm, 0.0).astype(BF16)
        li = lax.broadcasted_iota(jnp.int32, (N_MIXERS, R, LANES), 2)
        ml_ref[...] = jnp.where(li == 0, NEG_BIG, 0.0)
        acc_ref[...] = jnp.zeros_like(acc_ref)
        carry_ref[...] = jnp.zeros_like(carry_ref)
        pad = jnp.zeros((LANES - SUBLANES, MIX_WIDTH), F32)
        kn = jnp.concatenate([kn_ref[...], pad], axis=0).astype(BF16)
        vn = jnp.concatenate([vn_ref[...], pad], axis=0).astype(BF16)
        kcol = lax.broadcasted_iota(jnp.int32, (R, LANES), 1)
        qq = lax.broadcasted_iota(jnp.int32, (R, LANES), 0) & (SUBLANES - 1)
        causal = kcol <= qq
        strict = kcol < qq
        lg = [lax.dot_general(qs_ref[m], kn[:, m * W:(m + 1) * W], NT_DIMS, preferred_element_type=F32)
              for m in range(N_MIXERS)]
        sel = rows4(seln_ref[...]) > 0.5
        online(0, jnp.where(jnp.logical_and(sel, causal), lg[0], -jnp.inf), vn[:, 0:W])
        block_partial(nblk, jnp.where(causal, lg[1], NEG_BIG), vn[:, W:2 * W], jnp.zeros((1, W), F32))
        stick(lg[2], strict, vn[:, 2 * W:3 * W], ust_ref[:LANES, :LANES])
        ckn = ckn_ref[...]
        bias = cqb_ref[...] - jnp.concatenate(
            [jnp.broadcast_to(ckn[h:h + 1, :], (SUBLANES, LANES)) for h in range(4)], axis=0)
        online(3, jnp.where(causal, lg[3] + bias, -jnp.inf), vn[:, 3 * W:4 * W])

    n = nblk - 1 - j
    kf = jnp.concatenate([ka_ref[...], kb_ref[...]], axis=0)
    kc = kf.astype(BF16)
    vc = jnp.concatenate([va_ref[...], vb_ref[...]], axis=0).astype(BF16)
    lg = [lax.dot_general(qs_ref[m], kc[:, m * W:(m + 1) * W], NT_DIMS, preferred_element_type=F32)
          for m in range(N_MIXERS)]
    sel = rows4(selp_ref[...]) > 0.5
    online(0, jnp.where(sel, lg[0], -jnp.inf), vc[:, 0:W])
    block_partial(n, lg[1], vc[:, W:2 * W], jnp.mean(kf[:, W:2 * W], axis=0, keepdims=True))
    stick(lg[2], None, vc[:, 2 * W:3 * W], ust_ref[...])
    ckp = ckp_ref[...]
    ck_rows = [jnp.broadcast_to(jnp.concatenate([ckp[0:1, h * page:(h + 1) * page],
                                                 ckp[1:2, h * page:(h + 1) * page]], axis=1), (SUBLANES, 2 * page))
               for h in range(4)]
    online(3, lg[3] + (cqb_ref[:, 0:1] - jnp.concatenate(ck_rows, axis=0)), vc[:, 3 * W:4 * W])

    @pl.when(j == nblk - 1)
    def _():
        li = lax.broadcasted_iota(jnp.int32, (R, LANES), 1)
        sc_all = jnp.zeros((R, LANES), F32)
        for n_ in range(nblk):
            sc_all = jnp.where(li == n_, bst_ref[n_][:, 2:3], sc_all)
        beaten = jnp.zeros((R, LANES), F32)
        for m_ in range(nblk):
            bm = bst_ref[m_][:, 2:3]
            wins = jnp.logical_or(bm > sc_all, jnp.logical_and(bm == sc_all, m_ < li))
            beaten = beaten + jnp.where(wins, 1.0, 0.0)
        chosen = jnp.where(jnp.logical_or(jnp.logical_and(li < nblk, beaten < float(MOBA_TOPK)), li == nblk), 1.0, 0.0)
        m_tot = jnp.full((R, 1), NEG_BIG, F32)
        for n_ in range(nblk + 1):
            m_tot = jnp.maximum(m_tot, jnp.where(chosen[:, n_:n_ + 1] > 0.5, bst_ref[n_][:, 0:1], NEG_BIG))
        l_tot = jnp.zeros((R, 1), F32)
        acc_b = jnp.zeros((R, W), F32)
        for n_ in range(nblk + 1):
            wgt = jnp.where(chosen[:, n_:n_ + 1] > 0.5, jnp.exp(bst_ref[n_][:, 0:1] - m_tot), 0.0)
            l_tot = l_tot + wgt * bst_ref[n_][:, 1:2]
            acc_b = acc_b + wgt * bacc_ref[n_]
        res = [acc_ref[0] * (1.0 / ml_ref[0, :, 1:2]), acc_b * (1.0 / l_tot), acc_ref[2],
               acc_ref[3] * (1.0 / ml_ref[3, :, 1:2])]
        for m in range(N_MIXERS):
            r = jnp.where(own, res[m], 0.0)
            o_ref[:, m * W:(m + 1) * W] = (r[0:SUBLANES] + r[SUBLANES:2 * SUBLANES]
                                           + r[2 * SUBLANES:3 * SUBLANES] + r[3 * SUBLANES:])


def _sattn_call(page_table, q, kn, vn, sel, ckp, cqb, ckn, ust, k_pool, v_pool, *, lp, pool_base):
    n_dec, n_pages = page_table.shape
    page = k_pool.shape[1]
    nblk = n_pages * page // MOBA_BLOCK
    ppb = MOBA_BLOCK // page
    assert ppb == 2
    past = n_pages * page
    tq = SUBLANES
    R = 4 * tq
    blk = lambda w: pl.BlockSpec((tq, w), lambda b, j, pt: (b, 0))
    pg = lambda t: pl.BlockSpec((None, page, MIX_WIDTH),
                                lambda b, j, pt: (pt[b, ppb * (nblk - 1 - j) + t] + pool_base, 0, 0))
    ckp4 = ckp.reshape(n_dec, nblk, ppb, 4 * page)
    grid_spec = pltpu.PrefetchScalarGridSpec(
        num_scalar_prefetch=1,
        grid=(n_dec, nblk),
        in_specs=[blk(MIX_WIDTH), blk(MIX_WIDTH), blk(MIX_WIDTH),
                  pl.BlockSpec((None, tq, MOBA_BLOCK), lambda b, j, pt: (b, 0, nblk - 1 - j)),
                  pl.BlockSpec((None, tq, LANES), lambda b, j, pt: (b, 0, past // LANES)),
                  pl.BlockSpec((None, None, ppb, 4 * page), lambda b, j, pt: (b, nblk - 1 - j, 0, 0)),
                  pl.BlockSpec((None, R, LANES), lambda b, j, pt: (b, 0, 0)),
                  pl.BlockSpec((None, SUBLANES, LANES), lambda b, j, pt: (b, 0, 0)),
                  pl.BlockSpec(ust.shape, lambda b, j, pt: (0, 0)),
                  pg(0), pg(1), pg(0), pg(1)],
        out_specs=blk(MIX_WIDTH),
        scratch_shapes=[pltpu.VMEM((N_MIXERS, R, BRANCH_WIDTH), BF16),
                        pltpu.VMEM((N_MIXERS, R, LANES), F32),
                        pltpu.VMEM((N_MIXERS, R, BRANCH_WIDTH), F32),
                        pltpu.VMEM((R, LANES), F32),
                        pltpu.VMEM((nblk + 1, R, LANES), F32),
                        pltpu.VMEM((nblk + 1, R, BRANCH_WIDTH), F32)],
    )
    return pl.pallas_call(
        functools.partial(_sattn_kernel, nblk=nblk, page=page),
        grid_spec=grid_spec,
        out_shape=jax.ShapeDtypeStruct((n_dec * tq, MIX_WIDTH), F32),
        compiler_params=pltpu.CompilerParams(dimension_semantics=("arbitrary", "arbitrary"),
                                             vmem_limit_bytes=VMEM_LIMIT),
        name="sample_attn",
    )(page_table, q, kn, vn, sel, sel, ckp4, cqb, ckn, ust, k_pool, k_pool, v_pool, v_pool)


def _rope_tables(pos):
    half = HEAD_DIM // 2
    inv = ROPE_THETA ** (-jnp.arange(half, dtype=F32) / half)
    ang = pos.astype(F32)[:, None] * inv[None, :]
    cos = jnp.cos(ang)
    sin = jnp.sin(ang)
    return jnp.tile(cos, (1, LANES // half)), jnp.tile(jnp.concatenate([-sin, sin], axis=1), (1, LANES // HEAD_DIM))


def _tri_consts():
    r = np.arange(MOBA_BLOCK)
    incl = (r[:, None] <= r[None, :]).astype(np.float32)
    strict_later = (r[:, None] > r[None, :]).astype(np.float32)
    return jnp.asarray(incl, BF16), jnp.asarray(strict_later, BF16)


def _row_cumsum_matrix(tm, seq_len):
    r = np.arange(tm)
    m = (r[None, :] <= r[:, None])
    if seq_len < tm:
        m = m & ((r[None, :] // seq_len) == (r[:, None] // seq_len))
    return jnp.asarray(m.astype(np.float32), BF16)


def kernel(x_prompt, x_sample, cache_k, cache_v, cache_kidx, cache_logf, state_conv, page_table,
           w_in, g_kidx, b_forget, w_branch, w_out, w_ffn_in, conv_w, conv_b, w_ffn_out,
           g_pre_mix, g_post_mix, g_pre_ffn, g_post_ffn):
    B, S, _ = x_prompt.shape
    DB, TS, _ = x_sample.shape
    depth = w_in.shape[0]
    n_pool, page = cache_k.shape[1], cache_k.shape[2]
    n_pages = page_table.shape[1]
    past = n_pages * page
    assert TS == SUBLANES and S % MOBA_BLOCK == 0 and page == LANES
    lp = -(-(past + TS) // MOBA_BLOCK) * MOBA_BLOCK

    cos_p, sin_p = _rope_tables(jnp.arange(S, dtype=jnp.int32))
    cos_s, sin_s = _rope_tables(jnp.tile(past + jnp.arange(TS, dtype=jnp.int32), DB))
    tri, ust = _tri_consts()
    ltri_p = _row_cumsum_matrix(PROJ_TM, S)
    ltri_s = _row_cumsum_matrix(PROJ_TM, TS)
    pr = np.arange(4 * page)
    head_prefix = jnp.asarray(((pr[:, None] // page == pr[None, :] // page)
                               & (pr[:, None] <= pr[None, :])).astype(np.float32), BF16)
    pgr = np.arange(n_pages)
    page_strict = jnp.asarray((pgr[None, :] < pgr[:, None]).astype(np.float32), BF16)

    k_pool = cache_k.reshape(depth * n_pool, page, MIX_WIDTH)
    v_pool = cache_v.reshape(depth * n_pool, page, MIX_WIDTH)
    kidx_pool = cache_kidx.reshape(depth * n_pool, page, IDX_DIM)
    logf_pool = cache_logf.astype(F32).transpose(0, 1, 3, 2).reshape(depth * n_pool, 4 * page)

    yp = x_prompt.reshape(B * S, D_MODEL)
    ys = x_sample.reshape(DB * TS, D_MODEL)
    row2 = lambda v: v.reshape(1, -1)
    outs_p = [[] for _ in range(5)]
    outs_s = [[] for _ in range(5)]
    for l in range(depth):
        w = w_in[l]
        o0 = 3 * MIX_WIDTH
        o1 = o0 + IDX_HEADS * IDX_DIM
        o2 = o1 + IDX_DIM
        o3 = o2 + IDX_HEADS
        o4 = o3 + 4
        wqkv = w[:, :o0].astype(BF16)
        wqi = w[:, o0:o1].astype(BF16)
        wsm = jnp.concatenate([w[:, o1:o2], w[:, o1:o2], w[:, o2:o3], w[:, o3:o4], w[:, o3:o4],
                               jnp.zeros((D_MODEL, LANES - 12), F32)], axis=1).astype(BF16)
        wg = w[:, o4:].astype(BF16)
        wbr = w_branch[l].astype(BF16)
        wout = w_out[l].astype(BF16)
        wa = w_ffn_in[l][:, :D_FF].astype(BF16)
        wb = w_ffn_in[l][:, D_FF:].astype(BF16)
        wo = w_ffn_out[l].astype(BF16)
        cw = jnp.concatenate([conv_w[l], jnp.zeros((SUBLANES - CONV_WIDTH, D_FF), F32)], axis=0)
        cb = row2(conv_b[l])
        gk = row2(jnp.concatenate([g_kidx[l], g_kidx[l]]))
        bfp = row2(jnp.concatenate([jnp.zeros((4,), F32), b_forget[l], b_forget[l], jnp.zeros((LANES - 12,), F32)]))
        gpre, gpost = row2(g_pre_mix[l]), row2(g_post_mix[l])
        gpre_f, gpost_f = row2(g_pre_ffn[l]), row2(g_post_ffn[l])

        q, kf, vf, kb, vb, qi, kif, kib, misc, kbar = _proj_call(
            yp, gpre, wqkv, wqi, wsm, cos_p, sin_p, gk, bfp, ltri_p, seq_len=S, q_dtype=BF16)
        ckt = jnp.pad(misc[:, 8:12].reshape(B, S, 4).transpose(0, 2, 1), ((0, 0), (0, SUBLANES - 4), (0, 0)))
        kbar_p = jnp.pad(kbar.reshape(B, S // MOBA_BLOCK, MIX_WIDTH).astype(BF16),
                         ((0, 0), (0, LANES - S // MOBA_BLOCK), (0, 0)))
        o = _pattn_call(q, qi, misc, kb, vb, kib, ckt, kbar_p, tri, ust, B=B, S=S)
        yp = _merge_call(yp, o, gpre, wg, wbr, wout, gpost)
        yp, alast = _ffn_call(yp, gpre_f, wa, wb, wo, cw, cb, gpost_f, seq_len=S)
        outs_p[0].append(kf.reshape(B, S, N_HEADS, HEAD_DIM))
        outs_p[1].append(vf.reshape(B, S, N_HEADS, HEAD_DIM))
        outs_p[2].append(kif[:, :IDX_DIM].reshape(B, S, IDX_DIM))
        outs_p[3].append(misc[:, 4:8].reshape(B, S, 4))
        alast = alast.reshape(B, -1, SUBLANES, D_FF)[:, -1]
        outs_p[4].append(alast[:, SUBLANES - (CONV_WIDTH - 1):, :])

        q, kf, vf, kb, vb, qi, kif, kib, misc, kbar = _proj_call(
            ys, gpre, wqkv, wqi, wsm, cos_s, sin_s, gk, bfp, ltri_s, seq_len=TS, q_dtype=F32)
        sel, ckp, cqb, ckn = _sidx_call(page_table, qi, misc, kif, tri, head_prefix, page_strict,
                                        kidx_pool, logf_pool, lp=lp, pool_base=l * n_pool)
        o = _sattn_call(page_table, q, kf, vf, sel, ckp, cqb, ckn, ust, k_pool, v_pool,
                        lp=lp, pool_base=l * n_pool)
        ys = _merge_call(ys, o, gpre, wg, wbr, wout, gpost)
        st = state_conv[l].astype(F32)
        zeros = jnp.zeros((DB, TS - 2, D_FF), F32)
        e1 = jnp.concatenate([st[:, 1:2], jnp.zeros((DB, TS - 1, D_FF), F32)], axis=1).reshape(DB * TS, D_FF)
        e2 = jnp.concatenate([st, zeros], axis=1).reshape(DB * TS, D_FF)
        ys, a_all = _ffn_call(ys, gpre_f, wa, wb, wo, cw, cb, gpost_f, seq_len=TS, state=(e1, e2))
        outs_s[0].append(kf.reshape(DB, TS, N_HEADS, HEAD_DIM))
        outs_s[1].append(vf.reshape(DB, TS, N_HEADS, HEAD_DIM))
        outs_s[2].append(kif[:, :IDX_DIM].reshape(DB, TS, IDX_DIM))
        outs_s[3].append(misc[:, 4:8].reshape(DB, TS, 4))
        outs_s[4].append(a_all.reshape(DB, TS, D_FF)[:, TS - (CONV_WIDTH - 1):, :])

    return (yp.reshape(B, S, D_MODEL), ys.reshape(DB, TS, D_MODEL),
            *[jnp.stack(v) for v in outs_p], *[jnp.stack(v) for v in outs_s])
```

```python
import functools

import numpy as np
import jax
import jax.numpy as jnp
from jax import lax
from jax.experimental import pallas as pl
from jax.experimental.pallas import tpu as pltpu

D_MODEL = 1024
HEAD_DIM = 64
N_HEADS = 16
MIX_WIDTH = N_HEADS * HEAD_DIM
BRANCH_WIDTH = 256
N_MIXERS = 4
IDX_HEADS = 4
IDX_DIM = 64
DSA_TOPK = 256
MOBA_BLOCK = 256
MOBA_TOPK = 3
ROPE_THETA = 10000.0
D_FF = 4 * D_MODEL
CONV_WIDTH = 3
NORM_EPS = 1e-6

LANES = 128
SUBLANES = 8
VMEM_LIMIT = 56 * 1024 * 1024

F32 = jnp.float32
BF16 = jnp.bfloat16
NT_DIMS = (((1,), (1,)), ((), ()))
NEG_BIG = -1e30
INT_MIN = -2147483648

PROJ_TM = 256
ATTN_TQ = 256
MERGE_TM = 256
FFN_TM = 512
FFN_TF = 1024


def _rms(x, g):
    return x * lax.rsqrt(jnp.mean(x * x, axis=-1, keepdims=True) + NORM_EPS) * g


def _softplus_neg_abs(x):
    return jnp.log1p(jnp.exp(-jnp.abs(x)))


def _split_dot(a, b_bf16):
    hi = a.astype(BF16)
    lo = (a - hi.astype(F32)).astype(BF16)
    return (jnp.dot(hi, b_bf16, preferred_element_type=F32)
            + jnp.dot(lo, b_bf16, preferred_element_type=F32))


def _split_dot_left(a_bf16, b):
    hi = b.astype(BF16)
    lo = (b - hi.astype(F32)).astype(BF16)
    return (jnp.dot(a_bf16, hi, preferred_element_type=F32)
            + jnp.dot(a_bf16, lo, preferred_element_type=F32))


def _sortable_key(score):
    bits = pltpu.bitcast(score + 0.0, jnp.int32)
    return bits ^ ((bits >> 31) & 0x7FFFFFFF)


def _kth_largest_key(key_ref, kth, rows):
    def body(it, lo):
        cand = lo + lax.shift_left(jnp.int32(1), 31 - it)
        cnt = jnp.sum(jnp.where(key_ref[...] >= cand, 1.0, 0.0), axis=1, keepdims=True)
        return jnp.where(cnt >= kth, cand, lo)
    return lax.fori_loop(0, 32, body, jnp.full((rows, 1), INT_MIN, jnp.int32))


def _topk_select(key_ref, tri_ref, kth, rows, width):
    thr = _kth_largest_key(key_ref, float(kth), rows)
    key = key_ref[...]
    gt = key > thr
    tie = key == thr
    need = float(kth) - jnp.sum(jnp.where(gt, 1.0, 0.0), axis=1, keepdims=True)
    off = jnp.zeros((rows, 1), F32)
    cw = tri_ref.shape[0]
    parts = []
    for c in range(width // cw):
        sl = slice(c * cw, (c + 1) * cw)
        tc = jnp.where(tie[:, sl], 1.0, 0.0).astype(BF16)
        within = jnp.dot(tc, tri_ref[...], preferred_element_type=F32)
        parts.append(jnp.logical_or(gt[:, sl], jnp.logical_and(tie[:, sl], within + off <= need)))
        off = off + within[:, cw - 1:cw]
    return jnp.concatenate(parts, axis=1)


def _proj_kernel(x_ref, g_ref, wqkv_ref, wqi_ref, wsm_ref, cos_ref, sin_ref, gk_ref, bf_ref, ltri_ref,
                 q_ref, kf_ref, vf_ref, kb_ref, vb_ref, qi_ref, kif_ref, kib_ref, misc_ref, kbar_ref,
                 carry_ref, *, tm, tiles_per_seq):
    i = pl.program_id(0)
    hb = _rms(x_ref[...], g_ref[...]).astype(BF16)
    cos = cos_ref[...]
    sin = sin_ref[...]
    lane = lax.broadcasted_iota(jnp.int32, (tm, LANES), 1)
    first_half = (lane & (HEAD_DIM - 1)) < HEAD_DIM // 2

    def rope(t):
        rot = jnp.where(first_half, pltpu.roll(t, LANES - HEAD_DIM // 2, 1), pltpu.roll(t, HEAD_DIM // 2, 1))
        return t * cos + rot * sin

    n_rot = 4
    for c in range(MIX_WIDTH // LANES):
        sl = slice(c * LANES, (c + 1) * LANES)
        qc = jnp.dot(hb, wqkv_ref[:, sl], preferred_element_type=F32)
        kc = jnp.dot(hb, wqkv_ref[:, MIX_WIDTH + c * LANES:MIX_WIDTH + (c + 1) * LANES],
                     preferred_element_type=F32)
        vc = jnp.dot(hb, wqkv_ref[:, 2 * MIX_WIDTH + c * LANES:2 * MIX_WIDTH + (c + 1) * LANES],
                     preferred_element_type=F32)
        if c < n_rot:
            qc = rope(qc)
            kc = rope(kc)
        q_ref[:, sl] = (qc * HEAD_DIM ** -0.5).astype(q_ref.dtype)
        kf_ref[:, sl] = kc
        kb_ref[:, sl] = kc.astype(BF16)
        vf_ref[:, sl] = vc
        vb_ref[:, sl] = vc.astype(BF16)
        kbar_ref[:, sl] = jnp.mean(kc, axis=0, keepdims=True)
    for c in range(IDX_HEADS * IDX_DIM // LANES):
        sl = slice(c * LANES, (c + 1) * LANES)
        qic = rope(jnp.dot(hb, wqi_ref[:, sl], preferred_element_type=F32))
        qi_ref[:, sl] = (qic * IDX_DIM ** -0.5).astype(qi_ref.dtype)

    sm = jnp.dot(hb, wsm_ref[...], preferred_element_type=F32)
    ki = sm[:, :LANES]
    ki = rope(_rms(ki, gk_ref[...]))
    kif_ref[...] = ki
    kib_ref[...] = ki.astype(BF16)

    mi = sm[:, LANES:]
    lf = mi + bf_ref[...]
    lf = jnp.minimum(lf, 0.0) - _softplus_neg_abs(lf)
    is_cum = jnp.logical_and(lane >= 8, lane < 12)
    cum = _split_dot_left(ltri_ref[...], jnp.where(is_cum, lf, 0.0))
    if tiles_per_seq > 1:
        @pl.when(i % tiles_per_seq == 0)
        def _():
            carry_ref[...] = jnp.zeros_like(carry_ref)
        cum = cum + carry_ref[0:1, :]
        carry_ref[0:1, :] = cum[tm - 1:tm, :]
    misc = jnp.where(lane < 4, mi * IDX_HEADS ** -0.5,
                     jnp.where(lane < 8, lf, jnp.where(lane < 12, cum, 0.0)))
    misc_ref[...] = misc


def _proj_call(x, g, wqkv, wqi, wsm, cos_t, sin_t, gk, bfp, ltri, *, seq_len, q_dtype):
    T = x.shape[0]
    tm = PROJ_TM
    tiles_per_seq = max(seq_len // tm, 1)
    n_pos_tiles = cos_t.shape[0] // tm
    row = lambda w: pl.BlockSpec((tm, w), lambda i: (i, 0))
    const = lambda a: pl.BlockSpec(a.shape, lambda i: (0,) * a.ndim)
    pos = pl.BlockSpec((tm, LANES), lambda i: (i % n_pos_tiles, 0))
    out_shapes = (
        jax.ShapeDtypeStruct((T, MIX_WIDTH), q_dtype),
        jax.ShapeDtypeStruct((T, MIX_WIDTH), F32),
        jax.ShapeDtypeStruct((T, MIX_WIDTH), F32),
        jax.ShapeDtypeStruct((T, MIX_WIDTH), BF16),
        jax.ShapeDtypeStruct((T, MIX_WIDTH), BF16),
        jax.ShapeDtypeStruct((T, IDX_HEADS * IDX_DIM), q_dtype),
        jax.ShapeDtypeStruct((T, LANES), F32),
        jax.ShapeDtypeStruct((T, LANES), BF16),
        jax.ShapeDtypeStruct((T, LANES), F32),
        jax.ShapeDtypeStruct((T // tm, 1, MIX_WIDTH), F32),
    )
    out_specs = (row(MIX_WIDTH), row(MIX_WIDTH), row(MIX_WIDTH), row(MIX_WIDTH), row(MIX_WIDTH),
                 row(IDX_HEADS * IDX_DIM), row(LANES), row(LANES), row(LANES),
                 pl.BlockSpec((None, 1, MIX_WIDTH), lambda i: (i, 0, 0)))
    return pl.pallas_call(
        functools.partial(_proj_kernel, tm=tm, tiles_per_seq=tiles_per_seq),
        grid=(T // tm,),
        in_specs=[row(D_MODEL), const(g), const(wqkv), const(wqi), const(wsm), pos, pos,
                  const(gk), const(bfp), const(ltri)],
        out_specs=out_specs,
        out_shape=out_shapes,
        scratch_shapes=[pltpu.VMEM((SUBLANES, LANES), F32)],
        compiler_params=pltpu.CompilerParams(dimension_semantics=("arbitrary",),
                                             vmem_limit_bytes=VMEM_LIMIT),
        name="proj",
    )(x, g, wqkv, wqi, wsm, cos_t, sin_t, gk, bfp, ltri)


ATTN_CK = MOBA_BLOCK
N_PAIRS = N_HEADS // 2
STICK_PAIRS = (4, 5)
PAIR_GROUPS = ((0, 1, 2, 3), (4, 5, 6, 7))
SOFTMAX_SLOTS = 4
BISECT_ROWS = 128


def _pattn_kernel(q_ref, qi_ref, misc_ref, k_ref, v_ref, ki_ref, ck_ref, kbar_ref, tri_ref, ust_ref,
                  o_ref,
                  qst_ref, qist_ref, key_ref, sb_ref, lg_ref, mx_ref, ls_ref, acc_ref, cbias_ref, mb_ref,
                  cqb_ref, carry_ref, *, tq, S):
    ck = ATTN_CK
    nck = S // ck
    R = 2 * tq
    i = pl.program_id(1)
    q0 = i * tq
    nc = q0 // ck + 1
    cur = q0 // MOBA_BLOCK
    qpos = q0 + lax.broadcasted_iota(jnp.int32, (tq, 1), 0)
    colv = lax.broadcasted_iota(jnp.int32, (tq, ck), 1)
    lane = lax.broadcasted_iota(jnp.int32, (tq, LANES), 1)
    lo_half = lane < HEAD_DIM
    lane_r = lax.broadcasted_iota(jnp.int32, (R, LANES), 1)
    misc = misc_ref[...]

    def both_rows(x):
        return jnp.concatenate([x, x], axis=0)

    def both_halves(x):
        return jnp.concatenate([x] * (ck // LANES), axis=1)

    for p in range(N_PAIRS):
        blk = q_ref[:, p * LANES:(p + 1) * LANES]
        zero = jnp.zeros_like(blk)
        qst_ref[p, :tq, :] = jnp.where(lo_half, blk, zero)
        qst_ref[p, tq:, :] = jnp.where(lo_half, zero, blk)
    for h in range(IDX_HEADS):
        blk = qi_ref[:, (h // 2) * LANES:(h // 2 + 1) * LANES]
        zero = jnp.zeros_like(blk)
        qist_ref[h * tq:(h + 1) * tq, :] = jnp.where(lo_half, blk, zero) if h % 2 == 0 else jnp.where(lo_half, zero, blk)
    acc_ref[...] = jnp.zeros_like(acc_ref)
    carry_ref[...] = jnp.zeros_like(carry_ref)

    bi8 = lax.broadcasted_iota(jnp.int32, (SUBLANES, R), 0)
    for pb in range(2):
        p = 2 + pb
        sc8 = lax.dot_general(kbar_ref[:SUBLANES, p * LANES:(p + 1) * LANES], qst_ref[p], NT_DIMS,
                              preferred_element_type=F32)
        beaten = jnp.zeros((SUBLANES, R), F32)
        for m_ in range(nck):
            bm = jnp.sum(jnp.where(bi8 == m_, sc8, 0.0), axis=0, keepdims=True)
            wins = jnp.logical_or(bm > sc8, jnp.logical_and(bm == sc8, m_ < bi8))
            beaten = beaten + jnp.where(jnp.logical_and(wins, m_ < cur), 1.0, 0.0)
        chosen = jnp.logical_or(jnp.logical_and(bi8 < cur, beaten < float(MOBA_TOPK)), bi8 == cur)
        cb_t = jnp.concatenate([jnp.where(chosen, 0.0, NEG_BIG),
                                jnp.full((LANES - SUBLANES, R), NEG_BIG, F32)], axis=0)
        cbias_ref[pb] = cb_t.T
    for pd in range(2):
        cqb_ref[pd, :tq, :] = jnp.broadcast_to(misc[:, 8 + 2 * pd:9 + 2 * pd], (tq, LANES))
        cqb_ref[pd, tq:, :] = jnp.broadcast_to(misc[:, 9 + 2 * pd:10 + 2 * pd], (tq, LANES))

    def score_body(c, carry):
        r0 = pl.multiple_of(c * ck, ck)
        rel = jnp.maximum(lax.dot_general(qist_ref[...], ki_ref[pl.ds(r0, ck), :], NT_DIMS,
                                          preferred_element_type=F32), 0.0)
        score = misc[:, 0:1] * rel[0:tq]
        for h in range(1, IDX_HEADS):
            score = score + misc[:, h:h + 1] * rel[h * tq:(h + 1) * tq]
        key_ref[c] = jnp.where(c * ck + colv <= qpos, _sortable_key(score), INT_MIN)
        return carry

    lax.fori_loop(0, nc, score_body, 0)

    rb = BISECT_ROWS

    def count_over_chunks(r_lo, pred):
        def body(c, acc):
            for g in range(ck // LANES):
                acc = acc + jnp.where(pred(key_ref[c, r_lo:r_lo + rb, g * LANES:(g + 1) * LANES]), 1.0, 0.0)
            return acc
        return jnp.sum(lax.fori_loop(0, nc, body, jnp.zeros((rb, LANES), F32)), axis=1, keepdims=True)

    thr_parts = []
    need_parts = []
    for r_lo in range(0, tq, rb):
        def bit_body(it, lo, r_lo=r_lo):
            cand = lo + lax.shift_left(jnp.int32(1), 31 - it)
            return jnp.where(count_over_chunks(r_lo, lambda k_: k_ >= cand) >= float(DSA_TOPK), cand, lo)

        thr_r = lax.fori_loop(0, 32, bit_body, jnp.full((rb, 1), INT_MIN, jnp.int32))
        thr_parts.append(thr_r)
        need_parts.append(float(DSA_TOPK) - count_over_chunks(r_lo, lambda k_: k_ > thr_r))
    thr = jnp.concatenate(thr_parts, axis=0)
    need = jnp.concatenate(need_parts, axis=0)

    def tie_body(c, off):
        key = key_ref[c]
        tie = key == thr
        pre = jnp.dot(jnp.where(tie, 1.0, 0.0).astype(BF16), tri_ref[...], preferred_element_type=F32) + off
        sel = jnp.logical_or(key > thr, jnp.logical_and(tie, pre <= need))
        sel = jnp.logical_and(sel, c * ck + colv <= qpos)
        sb_ref[c] = jnp.where(sel, 0.0, NEG_BIG)
        return pre[:, ck - 1:ck]

    lax.fori_loop(0, nc, tie_body, jnp.zeros((tq, 1), F32))

    def run_group(pairs):
        soft = [p for p in pairs if p not in STICK_PAIRS]
        slot = {p: n for n, p in enumerate(soft)}
        for p in soft:
            mx_ref[slot[p]] = jnp.full((R, ck), -3e38, F32)
            ls_ref[slot[p]] = jnp.zeros((R, ck), F32)

        def main_body(idx, carry):
            c = nc - 1 - idx
            r0 = pl.multiple_of(c * ck, ck)
            kpos = c * ck + colv
            cz2 = both_rows(jnp.where(kpos <= qpos, 0.0, NEG_BIG))
            strict2 = both_rows(kpos < qpos)
            for p in pairs:
                lg = lax.dot_general(qst_ref[p], k_ref[pl.ds(r0, ck), p * LANES:(p + 1) * LANES], NT_DIMS,
                                     preferred_element_type=F32)
                if p in STICK_PAIRS:
                    z = lg
                    sp = jnp.maximum(z, 0.0) + jnp.log(1.0 + jnp.exp(-jnp.abs(z)))
                    log_keep = jnp.where(strict2, -sp, 0.0)
                    cr = carry_ref[p - 4]
                    later = _split_dot(log_keep, ust_ref[...]) + both_halves(cr)
                    carry_ref[p - 4] = cr + jnp.sum(log_keep, axis=1, keepdims=True)
                    w = jnp.where(strict2, jnp.exp(z - sp + later), 0.0)
                    acc_ref[p] += jnp.dot(w.astype(BF16), v_ref[pl.ds(r0, ck), p * LANES:(p + 1) * LANES],
                                          preferred_element_type=F32)
                    continue
                if p < 2:
                    lg = lg + both_rows(sb_ref[c])
                elif p < 4:
                    col = jnp.sum(jnp.where(lane_r == c, cbias_ref[p - 2], 0.0), axis=1, keepdims=True)
                    lg = lg + col + cz2
                else:
                    pd = p - 6
                    ckr = ck_ref[c]
                    ck_rows = jnp.concatenate([jnp.broadcast_to(ckr[2 * pd:2 * pd + 1, :], (tq, ck)),
                                               jnp.broadcast_to(ckr[2 * pd + 1:2 * pd + 2, :], (tq, ck))], axis=0)
                    lg = lg + (both_halves(cqb_ref[pd]) - ck_rows) + cz2
                lg_ref[slot[p], c] = lg
                mx_ref[slot[p]] = jnp.maximum(mx_ref[slot[p]], lg)
            return carry

        lax.fori_loop(0, nc, main_body, 0)

        for p in soft:
            mb_ref[slot[p]] = jnp.broadcast_to(jnp.max(mx_ref[slot[p]], axis=1, keepdims=True), (R, LANES))

        def pv_body(c, carry):
            r0 = pl.multiple_of(c * ck, ck)
            for p in soft:
                pr = jnp.exp(lg_ref[slot[p], c] - both_halves(mb_ref[slot[p]]))
                ls_ref[slot[p]] += pr
                acc_ref[p] += jnp.dot(pr.astype(BF16), v_ref[pl.ds(r0, ck), p * LANES:(p + 1) * LANES],
                                      preferred_element_type=F32)
            return carry

        lax.fori_loop(0, nc, pv_body, 0)

        for p in pairs:
            a = acc_ref[p]
            if p in slot:
                a = a * (1.0 / jnp.sum(ls_ref[slot[p]], axis=1, keepdims=True))
            o_ref[:, p * LANES:(p + 1) * LANES] = jnp.where(lo_half, a[:tq], a[tq:]).astype(o_ref.dtype)

    for pairs in PAIR_GROUPS:
        run_group(pairs)


def _pattn_call(q, qi, misc, kb, vb, kib, ck4, kbar, tri, ust, *, B, S):
    tq = ATTN_TQ
    ck = ATTN_CK
    nq = S // tq
    nck = S // ck
    T = B * S
    R = 2 * tq
    n_soft = SOFTMAX_SLOTS
    qrow = lambda w: pl.BlockSpec((tq, w), lambda b, i: (b * nq + i, 0))
    seq = lambda w: pl.BlockSpec((S, w), lambda b, i: (b, 0), pipeline_mode=pl.Buffered(1))
    const = lambda a: pl.BlockSpec(a.shape, lambda b, i: (0,) * a.ndim)
    return pl.pallas_call(
        functools.partial(_pattn_kernel, tq=tq, S=S),
        grid=(B, nq),
        in_specs=[qrow(MIX_WIDTH), qrow(IDX_HEADS * IDX_DIM), qrow(LANES), seq(MIX_WIDTH), seq(MIX_WIDTH),
                  seq(LANES), pl.BlockSpec((None, nck, SUBLANES, ck), lambda b, i: (b, 0, 0, 0)),
                  pl.BlockSpec((None, LANES, MIX_WIDTH), lambda b, i: (b, 0, 0)), const(tri), const(ust)],
        out_specs=qrow(MIX_WIDTH),
        out_shape=jax.ShapeDtypeStruct((T, MIX_WIDTH), BF16),
        scratch_shapes=[pltpu.VMEM((N_PAIRS, R, LANES), BF16),
                        pltpu.VMEM((IDX_HEADS * tq, LANES), BF16),
                        pltpu.VMEM((nck, tq, ck), jnp.int32),
                        pltpu.VMEM((nck, tq, ck), F32),
                        pltpu.VMEM((n_soft, nck, R, ck), F32),
                        pltpu.VMEM((n_soft, R, ck), F32),
                        pltpu.VMEM((n_soft, R, ck), F32),
                        pltpu.VMEM((N_PAIRS, R, LANES), F32),
                        pltpu.VMEM((2, R, LANES), F32),
                        pltpu.VMEM((n_soft, R, LANES), F32),
                        pltpu.VMEM((2, R, LANES), F32),
                        pltpu.VMEM((2, R, LANES), F32)],
        compiler_params=pltpu.CompilerParams(dimension_semantics=("arbitrary", "arbitrary"),
                                             vmem_limit_bytes=VMEM_LIMIT),
        name="prompt_attn",
    )(q, qi, misc, kb, vb, kib, ck4, kbar, tri, ust)


def _merge_kernel(x_ref, o_ref, gpre_ref, wg_ref, wbr_ref, wout_ref, gpost_ref, y_ref):
    x = x_ref[...]
    hb = _rms(x, gpre_ref[...]).astype(BF16)
    mix = None
    for m in range(N_MIXERS):
        gl = jnp.dot(hb, wg_ref[:, m * D_MODEL:(m + 1) * D_MODEL], preferred_element_type=F32)
        br = jnp.dot(o_ref[:, m * BRANCH_WIDTH:(m + 1) * BRANCH_WIDTH].astype(BF16), wbr_ref[m],
                     preferred_element_type=F32)
        term = jax.nn.sigmoid(gl) * br
        mix = term if mix is None else mix + term
    out = jnp.dot(mix.astype(BF16), wout_ref[...], preferred_element_type=F32)
    y_ref[...] = x + _rms(out, gpost_ref[...])


def _merge_call(x, o, gpre, wg, wbr, wout, gpost):
    T = x.shape[0]
    tm = MERGE_TM
    row = lambda w: pl.BlockSpec((tm, w), lambda i: (i, 0))
    const = lambda a: pl.BlockSpec(a.shape, lambda i: (0,) * a.ndim)
    return pl.pallas_call(
        _merge_kernel,
        grid=(T // tm,),
        in_specs=[row(D_MODEL), row(MIX_WIDTH), const(gpre), const(wg), const(wbr), const(wout), const(gpost)],
        out_specs=row(D_MODEL),
        out_shape=jax.ShapeDtypeStruct((T, D_MODEL), F32),
        compiler_params=pltpu.CompilerParams(dimension_semantics=("arbitrary",),
                                             vmem_limit_bytes=VMEM_LIMIT),
        name="merge",
    )(x, o, gpre, wg, wbr, wout, gpost)


def _ffn_kernel(*refs, tm, tiles_per_seq, has_state):
    if has_state:
        (x_ref, gpre_ref, wa_ref, wb_ref, wo_ref, cw_ref, cb_ref, gpost_ref, e1_ref, e2_ref,
         y_ref, a_ref, hb_ref, acc_ref) = refs
    else:
        (x_ref, gpre_ref, wa_ref, wb_ref, wo_ref, cw_ref, cb_ref, gpost_ref,
         y_ref, a_ref, hb_ref, acc_ref, prev_ref) = refs
    i = pl.program_id(0)
    f = pl.program_id(1)
    nf = pl.num_programs(1)

    @pl.when(f == 0)
    def _():
        hb_ref[...] = _rms(x_ref[...], gpre_ref[...]).astype(BF16)
        acc_ref[...] = jnp.zeros_like(acc_ref)

    hb = hb_ref[...]
    a = jnp.dot(hb, wa_ref[...], preferred_element_type=F32)
    b = jnp.dot(hb, wb_ref[...], preferred_element_type=F32)
    rowi = lax.broadcasted_iota(jnp.int32, a.shape, 0)
    r1 = pltpu.roll(a, 1, 0)
    r2 = pltpu.roll(a, 2, 0)
    if has_state:
        t = rowi & (SUBLANES - 1)
        p1 = jnp.where(t >= 1, r1, e1_ref[...])
        p2 = jnp.where(t >= 2, r2, e2_ref[...])
        a_ref[...] = a
    else:
        @pl.when(i % tiles_per_seq == 0)
        def _():
            prev_ref[f] = jnp.zeros(prev_ref.shape[1:], F32)
        prev = prev_ref[f]
        p1 = jnp.where(rowi >= 1, r1, prev[SUBLANES - 1:SUBLANES, :])
        p2 = jnp.where(rowi >= 2, r2, jnp.where(rowi == 0, prev[SUBLANES - 2:SUBLANES - 1, :],
                                                 prev[SUBLANES - 1:SUBLANES, :]))
        last = a[tm - SUBLANES:, :]
        prev_ref[f] = last
        a_ref[...] = last
    c = cb_ref[...] + p2 * cw_ref[0:1, :] + p1 * cw_ref[1:2, :] + a * cw_ref[2:3, :]
    gated = (jax.nn.gelu(c, approximate=True) * b).astype(BF16)
    acc_ref[...] += jnp.dot(gated, wo_ref[...], preferred_element_type=F32)

    @pl.when(f == nf - 1)
    def _():
        y_ref[...] = x_ref[...] + _rms(acc_ref[...], gpost_ref[...])


def _ffn_call(x, gpre, wa, wb, wo, cw, cb, gpost, *, seq_len, state=None):
    T = x.shape[0]
    tm = min(FFN_TM, T)
    tf = FFN_TF
    nf = D_FF // tf
    has_state = state is not None
    tiles_per_seq = max(seq_len // tm, 1)
    in_specs = [pl.BlockSpec((tm, D_MODEL), lambda i, f: (i, 0)),
                pl.BlockSpec((1, D_MODEL), lambda i, f: (0, 0)),
                pl.BlockSpec((D_MODEL, tf), lambda i, f: (0, f)),
                pl.BlockSpec((D_MODEL, tf), lambda i, f: (0, f)),
                pl.BlockSpec((tf, D_MODEL), lambda i, f: (f, 0)),
                pl.BlockSpec((SUBLANES, tf), lambda i, f: (0, f)),
                pl.BlockSpec((1, tf), lambda i, f: (0, f)),
                pl.BlockSpec((1, D_MODEL), lambda i, f: (0, 0))]
    args = [x, gpre, wa, wb, wo, cw, cb, gpost]
    scratch = [pltpu.VMEM((tm, D_MODEL), BF16), pltpu.VMEM((tm, D_MODEL), F32)]
    if has_state:
        in_specs += [pl.BlockSpec((tm, tf), lambda i, f: (i, f))] * 2
        args += list(state)
        a_shape = jax.ShapeDtypeStruct((T, D_FF), F32)
        a_spec = pl.BlockSpec((tm, tf), lambda i, f: (i, f))
    else:
        a_shape = jax.ShapeDtypeStruct((T // tm, SUBLANES, D_FF), F32)
        a_spec = pl.BlockSpec((None, SUBLANES, tf), lambda i, f: (i, 0, f))
        scratch.append(pltpu.VMEM((nf, SUBLANES, tf), F32))
    return pl.pallas_call(
        functools.partial(_ffn_kernel, tm=tm, tiles_per_seq=tiles_per_seq, has_state=has_state),
        grid=(T // tm, nf),
        in_specs=in_specs,
        out_specs=(pl.BlockSpec((tm, D_MODEL), lambda i, f: (i, 0)), a_spec),
        out_shape=(jax.ShapeDtypeStruct((T, D_MODEL), F32), a_shape),
        scratch_shapes=scratch,
        compiler_params=pltpu.CompilerParams(dimension_semantics=("arbitrary", "arbitrary"),
                                             vmem_limit_bytes=VMEM_LIMIT),
        name="ffn",
    )(*args)


def _sidx_kernel(pt_ref, qi_ref, misc_ref, kin_ref, tri_ref, hp_ref, ls_ref, kidx_hbm, logf_hbm,
                 sel_ref, ckp_ref, cqb_ref, ckn_ref, kbuf, lbuf, key_ref, sem, *, n_pages, page, lp, pool_base):
    b = pl.program_id(0)
    n_dec = pl.num_programs(0)
    past = n_pages * page
    tq = SUBLANES
    slot = b % 2

    def copies(seq, slot_, p):
        pg = pt_ref[seq, p] + pool_base
        return (pltpu.make_async_copy(kidx_hbm.at[pg], kbuf.at[slot_, :, pl.ds(p * page, page)], sem.at[slot_, 0]),
                pltpu.make_async_copy(logf_hbm.at[pl.ds(pg, 1)], lbuf.at[slot_, pl.ds(p, 1)], sem.at[slot_, 1]))

    def start_all(seq, slot_):
        for p in range(n_pages):
            for cp in copies(seq, slot_, p):
                cp.start()

    @pl.when(b == 0)
    def _():
        start_all(0, 0)

    @pl.when(b + 1 < n_dec)
    def _():
        start_all(b + 1, 1 - slot)

    for p in range(n_pages):
        for cp in copies(b, slot, p):
            cp.wait()

    qi = qi_ref[...]
    qall = jnp.concatenate([qi[:, h * IDX_DIM:(h + 1) * IDX_DIM] for h in range(IDX_HEADS)], axis=0).astype(BF16)
    rel = jnp.maximum(jnp.dot(qall, kbuf[slot].astype(BF16), preferred_element_type=F32), 0.0)
    knew = jnp.concatenate([kin_ref[:, :IDX_DIM], jnp.zeros((LANES - tq, IDX_DIM), F32)], axis=0).astype(BF16)
    reln = jnp.maximum(lax.dot_general(qall, knew, NT_DIMS, preferred_element_type=F32), 0.0)
    misc = misc_ref[...]
    sp = jnp.zeros((tq, past), F32)
    sn = jnp.zeros((tq, LANES), F32)
    for h in range(IDX_HEADS):
        sp = sp + misc[:, h:h + 1] * rel[h * tq:(h + 1) * tq, :]
        sn = sn + misc[:, h:h + 1] * reln[h * tq:(h + 1) * tq, :]
    qrow = lax.broadcasted_iota(jnp.int32, (tq, LANES), 0)
    ncol = lax.broadcasted_iota(jnp.int32, (tq, LANES), 1)
    key_ref[:, :past] = _sortable_key(sp)
    key_ref[:, past:past + LANES] = jnp.where(ncol <= qrow, _sortable_key(sn), INT_MIN)
    if lp > past + LANES:
        key_ref[:, past + LANES:] = jnp.full((tq, lp - past - LANES), INT_MIN, jnp.int32)
    sel = _topk_select(key_ref, tri_ref, DSA_TOPK, tq, lp)
    col = lax.broadcasted_iota(jnp.int32, (tq, lp), 1)
    qr = lax.broadcasted_iota(jnp.int32, (tq, lp), 0)
    sel_ref[...] = jnp.where(jnp.logical_and(sel, col <= past + qr), 1.0, 0.0)

    lg = lbuf[slot]
    within = _split_dot(lg, hp_ref[...])
    totals = jnp.concatenate(
        [jnp.broadcast_to(within[:, (h + 1) * page - 1:(h + 1) * page], (n_pages, page)) for h in range(4)], axis=1)
    offs = _split_dot_left(ls_ref[...], totals)
    ckp = within + offs
    ckp_ref[...] = ckp
    tot = ckp[n_pages - 1:n_pages, :]
    cq_rows = []
    ckn_rows = []
    mt = jnp.concatenate([misc, jnp.zeros((LANES - tq, LANES), F32)], axis=0).T
    for h in range(4):
        tot_h = tot[:, (h + 1) * page - 1:(h + 1) * page]
        cq_rows.append(jnp.broadcast_to(misc[:, 8 + h:9 + h] + tot_h, (tq, LANES)))
        ckn_rows.append(mt[8 + h:9 + h, :] + tot_h)
    cqb_ref[...] = jnp.concatenate(cq_rows, axis=0)
    ckn_ref[...] = jnp.concatenate(ckn_rows + [jnp.zeros((SUBLANES - 4, LANES), F32)], axis=0)


def _sidx_call(page_table, qi, misc, kif, tri, hp, ls, kidx_pool_t, logf_pool, *, lp, pool_base):
    n_dec, n_pages = page_table.shape
    page = kidx_pool_t.shape[2]
    tq = SUBLANES
    blk = lambda w: pl.BlockSpec((tq, w), lambda b, pt: (b, 0))
    const = lambda a: pl.BlockSpec(a.shape, lambda b, pt: (0,) * a.ndim)
    grid_spec = pltpu.PrefetchScalarGridSpec(
        num_scalar_prefetch=1,
        grid=(n_dec,),
        in_specs=[blk(IDX_HEADS * IDX_DIM), blk(LANES), blk(LANES), const(tri), const(hp), const(ls),
                  pl.BlockSpec(memory_space=pl.ANY), pl.BlockSpec(memory_space=pl.ANY)],
        out_specs=(pl.BlockSpec((None, tq, lp), lambda b, pt: (b, 0, 0)),
                   pl.BlockSpec((None, n_pages, 4 * page), lambda b, pt: (b, 0, 0)),
                   pl.BlockSpec((None, 4 * tq, LANES), lambda b, pt: (b, 0, 0)),
                   pl.BlockSpec((None, SUBLANES, LANES), lambda b, pt: (b, 0, 0))),
        scratch_shapes=[pltpu.VMEM((2, IDX_DIM, n_pages * page), F32),
                        pltpu.VMEM((2, n_pages, 4 * page), F32),
                        pltpu.VMEM((tq, lp), jnp.int32),
                        pltpu.SemaphoreType.DMA((2, 2))],
    )
    return pl.pallas_call(
        functools.partial(_sidx_kernel, n_pages=n_pages, page=page, lp=lp, pool_base=pool_base),
        grid_spec=grid_spec,
        out_shape=(jax.ShapeDtypeStruct((n_dec, tq, lp), F32),
                   jax.ShapeDtypeStruct((n_dec, n_pages, 4 * page), F32),
                   jax.ShapeDtypeStruct((n_dec, 4 * tq, LANES), F32),
                   jax.ShapeDtypeStruct((n_dec, SUBLANES, LANES), F32)),
        compiler_params=pltpu.CompilerParams(dimension_semantics=("arbitrary",),
                                             vmem_limit_bytes=VMEM_LIMIT),
        name="sample_index",
    )(page_table, qi, misc, kif, tri, hp, ls, kidx_pool_t, logf_pool)


SATTN_BLOCKS_PER_STEP = 2


def _sattn_kernel(pt_ref, q_ref, kn_ref, vn_ref, selp_ref, seln_ref, ckp_ref, cqb_ref, ckn_ref, ust_ref,
                  *rest, nblk, page):
    npg = SATTN_BLOCKS_PER_STEP * (MOBA_BLOCK // page)
    k_refs = rest[:npg]
    v_refs = rest[npg:2 * npg]
    o_ref, qs_ref, ml_ref, acc_ref, carry_ref, bst_ref, bacc_ref = rest[2 * npg:]
    j = pl.program_id(1)
    nsteps = pl.num_programs(1)
    R = 4 * SUBLANES
    W = BRANCH_WIDTH
    rowi = lax.broadcasted_iota(jnp.int32, (R, W), 0)
    lanei = lax.broadcasted_iota(jnp.int32, (R, W), 1)
    own = (rowi // SUBLANES) == (lanei // HEAD_DIM)
    li = lax.broadcasted_iota(jnp.int32, (R, LANES), 1)

    def online(m_idx, lg, pv):
        m_old = ml_ref[m_idx, :, 0:1]
        l_old = ml_ref[m_idx, :, 1:2]
        m_new = jnp.maximum(m_old, jnp.max(lg, axis=1, keepdims=True))
        alpha = jnp.exp(m_old - m_new)
        p = jnp.exp(lg - m_new)
        ml_ref[m_idx, :, 0:1] = m_new
        ml_ref[m_idx, :, 1:2] = alpha * l_old + jnp.sum(p, axis=1, keepdims=True)
        acc_ref[m_idx] = alpha * acc_ref[m_idx] + pv(p.astype(BF16))

    def block_partial(n, lg, pv, score):
        m_n = jnp.max(lg, axis=1, keepdims=True)
        p = jnp.exp(lg - m_n)
        l_n = jnp.sum(p, axis=1, keepdims=True)
        here = li == n
        bst_ref[0] = jnp.where(here, m_n, bst_ref[0])
        bst_ref[1] = jnp.where(here, l_n, bst_ref[1])
        bst_ref[2] = jnp.where(here, score, bst_ref[2])
        bacc_ref[n] = pv(p.astype(BF16))

    def stick(z, strict, pv, ust):
        sp = jnp.maximum(z, 0.0) + jnp.log(1.0 + jnp.exp(-jnp.abs(z)))
        log_keep = -sp
        if strict is not None:
            log_keep = jnp.where(strict, log_keep, 0.0)
        carry = carry_ref[:, 0:1]
        later = _split_dot(log_keep, ust) + carry
        carry_ref[:, 0:1] = carry + jnp.sum(log_keep, axis=1, keepdims=True)
        w = jnp.exp(z - sp + later)
        if strict is not None:
            w = jnp.where(strict, w, 0.0)
        acc_ref[2] = acc_ref[2] + pv(w.astype(BF16))

    def rows4(x8):
        return jnp.concatenate([x8] * 4, axis=0)

    @pl.when(j == 0)
    def _():
        q = q_ref[...]
        for m in range(N_MIXERS):
            qm = rows4(q[:, m * W:(m + 1) * W])
            qs_ref[m] = jnp.where(own, qm, 0.0).astype(BF16)
        li3 = lax.broadcasted_iota(jnp.int32, (N_MIXERS, R, LANES), 2)
        ml_ref[...] = jnp.where(li3 == 0, NEG_BIG, 0.0)
        acc_ref[...] = jnp.zeros_like(acc_ref)
        carry_ref[...] = jnp.zeros_like(carry_ref)
        bst_ref[...] = jnp.zeros_like(bst_ref)
        pad = jnp.zeros((LANES - SUBLANES, MIX_WIDTH), F32)
        kn = jnp.concatenate([kn_ref[...], pad], axis=0).astype(BF16)
        vn = jnp.concatenate([vn_ref[...], pad], axis=0).astype(BF16)
        kcol = lax.broadcasted_iota(jnp.int32, (R, LANES), 1)
        qq = lax.broadcasted_iota(jnp.int32, (R, LANES), 0) & (SUBLANES - 1)
        causal = kcol <= qq
        strict = kcol < qq
        lg = [lax.dot_general(qs_ref[m], kn[:, m * W:(m + 1) * W], NT_DIMS, preferred_element_type=F32)
              for m in range(N_MIXERS)]
        pvn = lambda m: (lambda p: jnp.dot(p, vn[:, m * W:(m + 1) * W], preferred_element_type=F32))
        sel = rows4(seln_ref[...]) > 0.5
        online(0, jnp.where(jnp.logical_and(sel, causal), lg[0], -jnp.inf), pvn(0))
        block_partial(nblk, jnp.where(causal, lg[1], NEG_BIG), pvn(1), jnp.zeros((R, 1), F32))
        stick(lg[2], strict, pvn(2), ust_ref[:LANES, :LANES])
        ckn = ckn_ref[...]
        bias = cqb_ref[...] - jnp.concatenate(
            [jnp.broadcast_to(ckn[h:h + 1, :], (SUBLANES, LANES)) for h in range(4)], axis=0)
        online(3, jnp.where(causal, lg[3] + bias, -jnp.inf), pvn(3))

    def past_block(n, kpages, vpages, sel8, ckp2):
        def kt(m):
            return jnp.concatenate([r[m * W:(m + 1) * W, :] for r in kpages], axis=1).astype(BF16)

        def pv(m):
            vt = jnp.concatenate([r[m * W:(m + 1) * W, :] for r in vpages], axis=1).astype(BF16)
            return lambda p: lax.dot_general(p, vt, NT_DIMS, preferred_element_type=F32)

        lg = [jnp.dot(qs_ref[m], kt(m), preferred_element_type=F32) for m in range(N_MIXERS)]
        online(0, jnp.where(rows4(sel8) > 0.5, lg[0], -jnp.inf), pv(0))
        block_partial(n, lg[1], pv(1), jnp.sum(lg[1], axis=1, keepdims=True) * (1.0 / MOBA_BLOCK))
        stick(lg[2], None, pv(2), ust_ref[...])
        ck_rows = [jnp.broadcast_to(jnp.concatenate([ckp2[t:t + 1, h * page:(h + 1) * page]
                                                     for t in range(len(kpages))], axis=1), (SUBLANES, MOBA_BLOCK))
                   for h in range(4)]
        online(3, lg[3] + (cqb_ref[:, 0:1] - jnp.concatenate(ck_rows, axis=0)), pv(3))

    ppb = MOBA_BLOCK // page
    selp = selp_ref[...]
    ckp = ckp_ref[...]
    for t in reversed(range(SATTN_BLOCKS_PER_STEP)):
        n = SATTN_BLOCKS_PER_STEP * (nsteps - 1 - j) + t
        past_block(n, k_refs[t * ppb:(t + 1) * ppb], v_refs[t * ppb:(t + 1) * ppb],
                   selp[:, t * MOBA_BLOCK:(t + 1) * MOBA_BLOCK], ckp[t * ppb:(t + 1) * ppb, :])

    @pl.when(j == nsteps - 1)
    def _():
        m_all = bst_ref[0]
        l_all = bst_ref[1]
        sc_all = bst_ref[2]
        beaten = jnp.zeros((R, LANES), F32)
        for m_ in range(nblk):
            bm = sc_all[:, m_:m_ + 1]
            wins = jnp.logical_or(bm > sc_all, jnp.logical_and(bm == sc_all, m_ < li))
            beaten = beaten + jnp.where(wins, 1.0, 0.0)
        chosen = jnp.logical_or(jnp.logical_and(li < nblk, beaten < float(MOBA_TOPK)), li == nblk)
        m_tot = jnp.max(jnp.where(chosen, m_all, NEG_BIG), axis=1, keepdims=True)
        wgt = jnp.where(chosen, jnp.exp(m_all - m_tot), 0.0)
        l_tot = jnp.sum(wgt * l_all, axis=1, keepdims=True)
        acc_b = jnp.zeros((R, W), F32)
        for n_ in range(nblk + 1):
            acc_b = acc_b + wgt[:, n_:n_ + 1] * bacc_ref[n_]
        res = [acc_ref[0] * (1.0 / ml_ref[0, :, 1:2]), acc_b * (1.0 / l_tot), acc_ref[2],
               acc_ref[3] * (1.0 / ml_ref[3, :, 1:2])]
        for m in range(N_MIXERS):
            r = jnp.where(own, res[m], 0.0)
            o_ref[:, m * W:(m + 1) * W] = (r[0:SUBLANES] + r[SUBLANES:2 * SUBLANES]
                                           + r[2 * SUBLANES:3 * SUBLANES] + r[3 * SUBLANES:])


def _sattn_call(page_table, q, kn, vn, sel, ckp, cqb, ckn, ust, k_pool_t, v_pool_t, *, lp, pool_base):
    n_dec, n_pages = page_table.shape
    page = k_pool_t.shape[2]
    nblk = n_pages * page // MOBA_BLOCK
    ppb = MOBA_BLOCK // page
    pps = ppb * SATTN_BLOCKS_PER_STEP
    nsteps = n_pages // pps
    assert nsteps * pps == n_pages
    past = n_pages * page
    tq = SUBLANES
    R = 4 * tq
    wstep = SATTN_BLOCKS_PER_STEP * MOBA_BLOCK
    blk = lambda w: pl.BlockSpec((tq, w), lambda b, j, pt: (b, 0))
    pg = lambda t: pl.BlockSpec((None, MIX_WIDTH, page),
                                lambda b, j, pt: (pt[b, pps * (nsteps - 1 - j) + t] + pool_base, 0, 0))
    ckp4 = ckp.reshape(n_dec, nsteps, pps, 4 * page)
    grid_spec = pltpu.PrefetchScalarGridSpec(
        num_scalar_prefetch=1,
        grid=(n_dec, nsteps),
        in_specs=[blk(MIX_WIDTH), blk(MIX_WIDTH), blk(MIX_WIDTH),
                  pl.BlockSpec((None, tq, wstep), lambda b, j, pt: (b, 0, nsteps - 1 - j)),
                  pl.BlockSpec((None, tq, LANES), lambda b, j, pt: (b, 0, past // LANES)),
                  pl.BlockSpec((None, None, pps, 4 * page), lambda b, j, pt: (b, nsteps - 1 - j, 0, 0)),
                  pl.BlockSpec((None, R, LANES), lambda b, j, pt: (b, 0, 0)),
                  pl.BlockSpec((None, SUBLANES, LANES), lambda b, j, pt: (b, 0, 0)),
                  pl.BlockSpec(ust.shape, lambda b, j, pt: (0, 0))]
                 + [pg(t) for t in range(pps)] * 2,
        out_specs=blk(MIX_WIDTH),
        scratch_shapes=[pltpu.VMEM((N_MIXERS, R, BRANCH_WIDTH), BF16),
                        pltpu.VMEM((N_MIXERS, R, LANES), F32),
                        pltpu.VMEM((N_MIXERS, R, BRANCH_WIDTH), F32),
                        pltpu.VMEM((R, LANES), F32),
                        pltpu.VMEM((3, R, LANES), F32),
                        pltpu.VMEM((nblk + 1, R, BRANCH_WIDTH), F32)],
    )
    return pl.pallas_call(
        functools.partial(_sattn_kernel, nblk=nblk, page=page),
        grid_spec=grid_spec,
        out_shape=jax.ShapeDtypeStruct((n_dec * tq, MIX_WIDTH), F32),
        compiler_params=pltpu.CompilerParams(dimension_semantics=("arbitrary", "arbitrary"),
                                             vmem_limit_bytes=VMEM_LIMIT),
        name="sample_attn",
    )(page_table, q, kn, vn, sel, sel, ckp4, cqb, ckn, ust, *([k_pool_t] * pps), *([v_pool_t] * pps))


def _rope_tables(pos):
    half = HEAD_DIM // 2
    inv = ROPE_THETA ** (-jnp.arange(half, dtype=F32) / half)
    ang = pos.astype(F32)[:, None] * inv[None, :]
    cos = jnp.cos(ang)
    sin = jnp.sin(ang)
    return jnp.tile(cos, (1, LANES // half)), jnp.tile(jnp.concatenate([-sin, sin], axis=1), (1, LANES // HEAD_DIM))


def _tri_consts():
    r = np.arange(MOBA_BLOCK)
    incl = (r[:, None] <= r[None, :]).astype(np.float32)
    strict_later = (r[:, None] > r[None, :]).astype(np.float32)
    return jnp.asarray(incl, BF16), jnp.asarray(strict_later, BF16)


def _row_cumsum_matrix(tm, seq_len):
    r = np.arange(tm)
    m = (r[None, :] <= r[:, None])
    if seq_len < tm:
        m = m & ((r[None, :] // seq_len) == (r[:, None] // seq_len))
    return jnp.asarray(m.astype(np.float32), BF16)


def kernel(x_prompt, x_sample, cache_k, cache_v, cache_kidx, cache_logf, state_conv, page_table,
           w_in, g_kidx, b_forget, w_branch, w_out, w_ffn_in, conv_w, conv_b, w_ffn_out,
           g_pre_mix, g_post_mix, g_pre_ffn, g_post_ffn):
    B, S, _ = x_prompt.shape
    DB, TS, _ = x_sample.shape
    depth = w_in.shape[0]
    n_pool, page = cache_k.shape[1], cache_k.shape[2]
    n_pages = page_table.shape[1]
    past = n_pages * page
    assert TS == SUBLANES and S % MOBA_BLOCK == 0 and page == LANES
    lp = -(-(past + TS) // MOBA_BLOCK) * MOBA_BLOCK

    cos_p, sin_p = _rope_tables(jnp.arange(S, dtype=jnp.int32))
    cos_s, sin_s = _rope_tables(jnp.tile(past + jnp.arange(TS, dtype=jnp.int32), DB))
    tri, ust = _tri_consts()
    ltri_p = _row_cumsum_matrix(PROJ_TM, S)
    ltri_s = _row_cumsum_matrix(PROJ_TM, TS)
    pr = np.arange(4 * page)
    head_prefix = jnp.asarray(((pr[:, None] // page == pr[None, :] // page)
                               & (pr[:, None] <= pr[None, :])).astype(np.float32), BF16)
    pgr = np.arange(n_pages)
    page_strict = jnp.asarray((pgr[None, :] < pgr[:, None]).astype(np.float32), BF16)

    k_pool = cache_k.transpose(0, 1, 3, 4, 2).reshape(depth * n_pool, MIX_WIDTH, page)
    v_pool = cache_v.transpose(0, 1, 3, 4, 2).reshape(depth * n_pool, MIX_WIDTH, page)
    kidx_pool = cache_kidx.transpose(0, 1, 3, 2).reshape(depth * n_pool, IDX_DIM, page)
    logf_pool = cache_logf.astype(F32).transpose(0, 1, 3, 2).reshape(depth * n_pool, 4 * page)

    yp = x_prompt.reshape(B * S, D_MODEL)
    ys = x_sample.reshape(DB * TS, D_MODEL)
    row2 = lambda v: v.reshape(1, -1)
    outs_p = [[] for _ in range(5)]
    outs_s = [[] for _ in range(5)]
    for l in range(depth):
        w = w_in[l]
        o0 = 3 * MIX_WIDTH
        o1 = o0 + IDX_HEADS * IDX_DIM
        o2 = o1 + IDX_DIM
        o3 = o2 + IDX_HEADS
        o4 = o3 + 4
        wqkv = w[:, :o0].astype(BF16)
        wqi = w[:, o0:o1].astype(BF16)
        wsm = jnp.concatenate([w[:, o1:o2], w[:, o1:o2], w[:, o2:o3], w[:, o3:o4], w[:, o3:o4],
                               jnp.zeros((D_MODEL, LANES - 12), F32)], axis=1).astype(BF16)
        wg = w[:, o4:].astype(BF16)
        wbr = w_branch[l].astype(BF16)
        wout = w_out[l].astype(BF16)
        wa = w_ffn_in[l][:, :D_FF].astype(BF16)
        wb = w_ffn_in[l][:, D_FF:].astype(BF16)
        wo = w_ffn_out[l].astype(BF16)
        cw = jnp.concatenate([conv_w[l], jnp.zeros((SUBLANES - CONV_WIDTH, D_FF), F32)], axis=0)
        cb = row2(conv_b[l])
        gk = row2(jnp.concatenate([g_kidx[l], g_kidx[l]]))
        bfp = row2(jnp.concatenate([jnp.zeros((4,), F32), b_forget[l], b_forget[l], jnp.zeros((LANES - 12,), F32)]))
        gpre, gpost = row2(g_pre_mix[l]), row2(g_post_mix[l])
        gpre_f, gpost_f = row2(g_pre_ffn[l]), row2(g_post_ffn[l])

        q, kf, vf, kb, vb, qi, kif, kib, misc, kbar = _proj_call(
            yp, gpre, wqkv, wqi, wsm, cos_p, sin_p, gk, bfp, ltri_p, seq_len=S, q_dtype=BF16)
        ckt = jnp.pad(misc[:, 8:12].reshape(B, S, 4).transpose(0, 2, 1), ((0, 0), (0, SUBLANES - 4), (0, 0)))
        ck4 = ckt.reshape(B, SUBLANES, S // ATTN_CK, ATTN_CK).transpose(0, 2, 1, 3)
        kbar_p = jnp.pad(kbar.reshape(B, S // MOBA_BLOCK, MIX_WIDTH).astype(BF16),
                         ((0, 0), (0, LANES - S // MOBA_BLOCK), (0, 0)))
        o = _pattn_call(q, qi, misc, kb, vb, kib, ck4, kbar_p, tri, ust, B=B, S=S)
        yp = _merge_call(yp, o, gpre, wg, wbr, wout, gpost)
        yp, alast = _ffn_call(yp, gpre_f, wa, wb, wo, cw, cb, gpost_f, seq_len=S)
        outs_p[0].append(kf.reshape(B, S, N_HEADS, HEAD_DIM))
        outs_p[1].append(vf.reshape(B, S, N_HEADS, HEAD_DIM))
        outs_p[2].append(kif[:, :IDX_DIM].reshape(B, S, IDX_DIM))
        outs_p[3].append(misc[:, 4:8].reshape(B, S, 4))
        alast = alast.reshape(B, -1, SUBLANES, D_FF)[:, -1]
        outs_p[4].append(alast[:, SUBLANES - (CONV_WIDTH - 1):, :])

        q, kf, vf, kb, vb, qi, kif, kib, misc, kbar = _proj_call(
            ys, gpre, wqkv, wqi, wsm, cos_s, sin_s, gk, bfp, ltri_s, seq_len=TS, q_dtype=F32)
        sel, ckp, cqb, ckn = _sidx_call(page_table, qi, misc, kif, tri, head_prefix, page_strict,
                                        kidx_pool, logf_pool, lp=lp, pool_base=l * n_pool)
        o = _sattn_call(page_table, q, kf, vf, sel, ckp, cqb, ckn, ust, k_pool, v_pool,
                        lp=lp, pool_base=l * n_pool)
        ys = _merge_call(ys, o, gpre, wg, wbr, wout, gpost)
        st = state_conv[l].astype(F32)
        zeros = jnp.zeros((DB, TS - 2, D_FF), F32)
        e1 = jnp.concatenate([st[:, 1:2], jnp.zeros((DB, TS - 1, D_FF), F32)], axis=1).reshape(DB * TS, D_FF)
        e2 = jnp.concatenate([st, zeros], axis=1).reshape(DB * TS, D_FF)
        ys, a_all = _ffn_call(ys, gpre_f, wa, wb, wo, cw, cb, gpost_f, seq_len=TS, state=(e1, e2))
        outs_s[0].append(kf.reshape(DB, TS, N_HEADS, HEAD_DIM))
        outs_s[1].append(vf.reshape(DB, TS, N_HEADS, HEAD_DIM))
        outs_s[2].append(kif[:, :IDX_DIM].reshape(DB, TS, IDX_DIM))
        outs_s[3].append(misc[:, 4:8].reshape(DB, TS, 4))
        outs_s[4].append(a_all.reshape(DB, TS, D_FF)[:, TS - (CONV_WIDTH - 1):, :])

    return (yp.reshape(B, S, D_MODEL), ys.reshape(DB, TS, D_MODEL),
            *[jnp.stack(v) for v in outs_p], *[jnp.stack(v) for v in outs_s])
```

```python
import functools

import numpy as np
import jax
import jax.numpy as jnp
from jax import lax
from jax.experimental import pallas as pl
from jax.experimental.pallas import tpu as pltpu

D_MODEL = 1024
HEAD_DIM = 64
N_HEADS = 16
MIX_WIDTH = N_HEADS * HEAD_DIM
BRANCH_WIDTH = 256
N_MIXERS = 4
IDX_HEADS = 4
IDX_DIM = 64
DSA_TOPK = 256
MOBA_BLOCK = 256
MOBA_TOPK = 3
ROPE_THETA = 10000.0
D_FF = 4 * D_MODEL
CONV_WIDTH = 3
NORM_EPS = 1e-6

LANES = 128
SUBLANES = 8
VMEM_LIMIT = 56 * 1024 * 1024

F32 = jnp.float32
BF16 = jnp.bfloat16
NT_DIMS = (((1,), (1,)), ((), ()))
NEG_BIG = -1e30
INT_MIN = -2147483648

PROJ_TM = 256
ATTN_TQ = 256
MERGE_TM = 256
FFN_TM = 512
FFN_TF = 1024


def _rms(x, g):
    return x * lax.rsqrt(jnp.mean(x * x, axis=-1, keepdims=True) + NORM_EPS) * g


def _softplus_neg_abs(x):
    return jnp.log1p(jnp.exp(-jnp.abs(x)))


def _split_dot(a, b_bf16):
    hi = a.astype(BF16)
    lo = (a - hi.astype(F32)).astype(BF16)
    return (jnp.dot(hi, b_bf16, preferred_element_type=F32)
            + jnp.dot(lo, b_bf16, preferred_element_type=F32))


def _split_dot_left(a_bf16, b):
    hi = b.astype(BF16)
    lo = (b - hi.astype(F32)).astype(BF16)
    return (jnp.dot(a_bf16, hi, preferred_element_type=F32)
            + jnp.dot(a_bf16, lo, preferred_element_type=F32))


def _sortable_key(score):
    bits = pltpu.bitcast(score + 0.0, jnp.int32)
    return bits ^ ((bits >> 31) & 0x7FFFFFFF)


def _kth_largest_key(key_ref, kth, rows):
    def body(it, lo):
        cand = lo + lax.shift_left(jnp.int32(1), 31 - it)
        cnt = jnp.sum(jnp.where(key_ref[...] >= cand, 1.0, 0.0), axis=1, keepdims=True)
        return jnp.where(cnt >= kth, cand, lo)
    return lax.fori_loop(0, 32, body, jnp.full((rows, 1), INT_MIN, jnp.int32))


def _topk_select(key_ref, tri_ref, kth, rows, width):
    thr = _kth_largest_key(key_ref, float(kth), rows)
    key = key_ref[...]
    gt = key > thr
    tie = key == thr
    need = float(kth) - jnp.sum(jnp.where(gt, 1.0, 0.0), axis=1, keepdims=True)
    off = jnp.zeros((rows, 1), F32)
    cw = tri_ref.shape[0]
    parts = []
    for c in range(width // cw):
        sl = slice(c * cw, (c + 1) * cw)
        tc = jnp.where(tie[:, sl], 1.0, 0.0).astype(BF16)
        within = jnp.dot(tc, tri_ref[...], preferred_element_type=F32)
        parts.append(jnp.logical_or(gt[:, sl], jnp.logical_and(tie[:, sl], within + off <= need)))
        off = off + within[:, cw - 1:cw]
    return jnp.concatenate(parts, axis=1)


def _proj_kernel(x_ref, g_ref, wqkv_ref, wqi_ref, wsm_ref, cos_ref, sin_ref, gk_ref, bf_ref, ltri_ref,
                 q_ref, kf_ref, vf_ref, kb_ref, vb_ref, qi_ref, kif_ref, kib_ref, misc_ref, kbar_ref,
                 carry_ref, *, tm, tiles_per_seq):
    i = pl.program_id(0)
    hb = _rms(x_ref[...], g_ref[...]).astype(BF16)
    cos = cos_ref[...]
    sin = sin_ref[...]
    lane = lax.broadcasted_iota(jnp.int32, (tm, LANES), 1)
    first_half = (lane & (HEAD_DIM - 1)) < HEAD_DIM // 2

    def rope(t):
        rot = jnp.where(first_half, pltpu.roll(t, LANES - HEAD_DIM // 2, 1), pltpu.roll(t, HEAD_DIM // 2, 1))
        return t * cos + rot * sin

    n_rot = 4
    for c in range(MIX_WIDTH // LANES):
        sl = slice(c * LANES, (c + 1) * LANES)
        qc = jnp.dot(hb, wqkv_ref[:, sl], preferred_element_type=F32)
        kc = jnp.dot(hb, wqkv_ref[:, MIX_WIDTH + c * LANES:MIX_WIDTH + (c + 1) * LANES],
                     preferred_element_type=F32)
        vc = jnp.dot(hb, wqkv_ref[:, 2 * MIX_WIDTH + c * LANES:2 * MIX_WIDTH + (c + 1) * LANES],
                     preferred_element_type=F32)
        if c < n_rot:
            qc = rope(qc)
            kc = rope(kc)
        q_ref[:, sl] = (qc * HEAD_DIM ** -0.5).astype(q_ref.dtype)
        kf_ref[:, sl] = kc
        kb_ref[:, sl] = kc.astype(BF16)
        vf_ref[:, sl] = vc
        vb_ref[:, sl] = vc.astype(BF16)
        kbar_ref[:, sl] = jnp.mean(kc, axis=0, keepdims=True)
    for c in range(IDX_HEADS * IDX_DIM // LANES):
        sl = slice(c * LANES, (c + 1) * LANES)
        qic = rope(jnp.dot(hb, wqi_ref[:, sl], preferred_element_type=F32))
        qi_ref[:, sl] = (qic * IDX_DIM ** -0.5).astype(qi_ref.dtype)

    sm = jnp.dot(hb, wsm_ref[...], preferred_element_type=F32)
    ki = sm[:, :LANES]
    ki = rope(_rms(ki, gk_ref[...]))
    kif_ref[...] = ki
    kib_ref[...] = ki.astype(BF16)

    mi = sm[:, LANES:]
    lf = mi + bf_ref[...]
    lf = jnp.minimum(lf, 0.0) - _softplus_neg_abs(lf)
    is_cum = jnp.logical_and(lane >= 8, lane < 12)
    cum = _split_dot_left(ltri_ref[...], jnp.where(is_cum, lf, 0.0))
    if tiles_per_seq > 1:
        @pl.when(i % tiles_per_seq == 0)
        def _():
            carry_ref[...] = jnp.zeros_like(carry_ref)
        cum = cum + carry_ref[0:1, :]
        carry_ref[0:1, :] = cum[tm - 1:tm, :]
    misc = jnp.where(lane < 4, mi * IDX_HEADS ** -0.5,
                     jnp.where(lane < 8, lf, jnp.where(lane < 12, cum, 0.0)))
    misc_ref[...] = misc


def _proj_call(x, g, wqkv, wqi, wsm, cos_t, sin_t, gk, bfp, ltri, *, seq_len, q_dtype):
    T = x.shape[0]
    tm = PROJ_TM
    tiles_per_seq = max(seq_len // tm, 1)
    n_pos_tiles = cos_t.shape[0] // tm
    row = lambda w: pl.BlockSpec((tm, w), lambda i: (i, 0))
    const = lambda a: pl.BlockSpec(a.shape, lambda i: (0,) * a.ndim)
    pos = pl.BlockSpec((tm, LANES), lambda i: (i % n_pos_tiles, 0))
    out_shapes = (
        jax.ShapeDtypeStruct((T, MIX_WIDTH), q_dtype),
        jax.ShapeDtypeStruct((T, MIX_WIDTH), F32),
        jax.ShapeDtypeStruct((T, MIX_WIDTH), F32),
        jax.ShapeDtypeStruct((T, MIX_WIDTH), BF16),
        jax.ShapeDtypeStruct((T, MIX_WIDTH), BF16),
        jax.ShapeDtypeStruct((T, IDX_HEADS * IDX_DIM), q_dtype),
        jax.ShapeDtypeStruct((T, LANES), F32),
        jax.ShapeDtypeStruct((T, LANES), BF16),
        jax.ShapeDtypeStruct((T, LANES), F32),
        jax.ShapeDtypeStruct((T // tm, 1, MIX_WIDTH), F32),
    )
    out_specs = (row(MIX_WIDTH), row(MIX_WIDTH), row(MIX_WIDTH), row(MIX_WIDTH), row(MIX_WIDTH),
                 row(IDX_HEADS * IDX_DIM), row(LANES), row(LANES), row(LANES),
                 pl.BlockSpec((None, 1, MIX_WIDTH), lambda i: (i, 0, 0)))
    return pl.pallas_call(
        functools.partial(_proj_kernel, tm=tm, tiles_per_seq=tiles_per_seq),
        grid=(T // tm,),
        in_specs=[row(D_MODEL), const(g), const(wqkv), const(wqi), const(wsm), pos, pos,
                  const(gk), const(bfp), const(ltri)],
        out_specs=out_specs,
        out_shape=out_shapes,
        scratch_shapes=[pltpu.VMEM((SUBLANES, LANES), F32)],
        compiler_params=pltpu.CompilerParams(dimension_semantics=("arbitrary",),
                                             vmem_limit_bytes=VMEM_LIMIT),
        name="proj",
    )(x, g, wqkv, wqi, wsm, cos_t, sin_t, gk, bfp, ltri)


ATTN_CK = MOBA_BLOCK
N_PAIRS = N_HEADS // 2
STICK_PAIRS = (4, 5)
PAIR_GROUPS = ((0, 1, 2, 3), (4, 5, 6, 7))
SOFTMAX_SLOTS = 4
BISECT_ROWS = 128


def _pattn_kernel(q_ref, qi_ref, misc_ref, k_ref, v_ref, ki_ref, ck_ref, kbar_ref, tri_ref, ust_ref,
                  o_ref,
                  qst_ref, qist_ref, key_ref, sb_ref, lg_ref, mx_ref, ls_ref, acc_ref, cbias_ref, mb_ref,
                  cqb_ref, carry_ref, *, tq, S):
    ck = ATTN_CK
    nck = S // ck
    R = 2 * tq
    i = pl.program_id(1)
    q0 = i * tq
    nc = q0 // ck + 1
    cur = q0 // MOBA_BLOCK
    qpos = q0 + lax.broadcasted_iota(jnp.int32, (tq, 1), 0)
    colv = lax.broadcasted_iota(jnp.int32, (tq, ck), 1)
    lane = lax.broadcasted_iota(jnp.int32, (tq, LANES), 1)
    lo_half = lane < HEAD_DIM
    lane_r = lax.broadcasted_iota(jnp.int32, (R, LANES), 1)
    misc = misc_ref[...]

    def both_rows(x):
        return jnp.concatenate([x, x], axis=0)

    def both_halves(x):
        return jnp.concatenate([x] * (ck // LANES), axis=1)

    for p in range(N_PAIRS):
        blk = q_ref[:, p * LANES:(p + 1) * LANES]
        zero = jnp.zeros_like(blk)
        qst_ref[p, :tq, :] = jnp.where(lo_half, blk, zero)
        qst_ref[p, tq:, :] = jnp.where(lo_half, zero, blk)
    for h in range(IDX_HEADS):
        blk = qi_ref[:, (h // 2) * LANES:(h // 2 + 1) * LANES]
        zero = jnp.zeros_like(blk)
        qist_ref[h * tq:(h + 1) * tq, :] = jnp.where(lo_half, blk, zero) if h % 2 == 0 else jnp.where(lo_half, zero, blk)
    acc_ref[...] = jnp.zeros_like(acc_ref)
    carry_ref[...] = jnp.zeros_like(carry_ref)

    bi8 = lax.broadcasted_iota(jnp.int32, (SUBLANES, R), 0)
    for pb in range(2):
        p = 2 + pb
        sc8 = lax.dot_general(kbar_ref[:SUBLANES, p * LANES:(p + 1) * LANES], qst_ref[p], NT_DIMS,
                              preferred_element_type=F32)
        beaten = jnp.zeros((SUBLANES, R), F32)
        for m_ in range(nck):
            bm = jnp.sum(jnp.where(bi8 == m_, sc8, 0.0), axis=0, keepdims=True)
            wins = jnp.logical_or(bm > sc8, jnp.logical_and(bm == sc8, m_ < bi8))
            beaten = beaten + jnp.where(jnp.logical_and(wins, m_ < cur), 1.0, 0.0)
        chosen = jnp.logical_or(jnp.logical_and(bi8 < cur, beaten < float(MOBA_TOPK)), bi8 == cur)
        cb_t = jnp.concatenate([jnp.where(chosen, 0.0, NEG_BIG),
                                jnp.full((LANES - SUBLANES, R), NEG_BIG, F32)], axis=0)
        cbias_ref[pb] = cb_t.T
    for pd in range(2):
        cqb_ref[pd, :tq, :] = jnp.broadcast_to(misc[:, 8 + 2 * pd:9 + 2 * pd], (tq, LANES))
        cqb_ref[pd, tq:, :] = jnp.broadcast_to(misc[:, 9 + 2 * pd:10 + 2 * pd], (tq, LANES))

    def score_body(c, carry):
        r0 = pl.multiple_of(c * ck, ck)
        rel = jnp.maximum(lax.dot_general(qist_ref[...], ki_ref[pl.ds(r0, ck), :], NT_DIMS,
                                          preferred_element_type=F32), 0.0)
        score = misc[:, 0:1] * rel[0:tq]
        for h in range(1, IDX_HEADS):
            score = score + misc[:, h:h + 1] * rel[h * tq:(h + 1) * tq]
        key_ref[c] = jnp.where(c * ck + colv <= qpos, _sortable_key(score), INT_MIN)
        return carry

    lax.fori_loop(0, nc, score_body, 0)

    rb = BISECT_ROWS

    def search(n_chunks):
        def count(r_lo, pred):
            acc = jnp.zeros((rb, LANES), F32)
            for c in range(n_chunks):
                for g in range(ck // LANES):
                    acc = acc + jnp.where(pred(key_ref[c, r_lo:r_lo + rb, g * LANES:(g + 1) * LANES]), 1.0, 0.0)
            return jnp.sum(acc, axis=1, keepdims=True)

        def run():
            starts = list(range(0, tq, rb))

            def bit_body(it, los):
                bit = lax.shift_left(jnp.int32(1), 31 - it)
                out = []
                for r_lo, lo in zip(starts, los):
                    cand = lo + bit
                    out.append(jnp.where(count(r_lo, lambda k_: k_ >= cand) >= float(DSA_TOPK), cand, lo))
                return tuple(out)

            thrs = lax.fori_loop(0, 32, bit_body,
                                 tuple(jnp.full((rb, 1), INT_MIN, jnp.int32) for _ in starts))
            needs = [float(DSA_TOPK) - count(r_lo, lambda k_: k_ > t) for r_lo, t in zip(starts, thrs)]
            return jnp.concatenate(thrs, axis=0), jnp.concatenate(needs, axis=0)
        return run

    thr, need = lax.switch(nc - 1, [search(n) for n in range(1, nck + 1)])

    def tie_body(c, off):
        key = key_ref[c]
        tie = key == thr
        pre = jnp.dot(jnp.where(tie, 1.0, 0.0).astype(BF16), tri_ref[...], preferred_element_type=F32) + off
        sel = jnp.logical_or(key > thr, jnp.logical_and(tie, pre <= need))
        sel = jnp.logical_and(sel, c * ck + colv <= qpos)
        sb_ref[c] = jnp.where(sel, 0.0, NEG_BIG)
        return pre[:, ck - 1:ck]

    lax.fori_loop(0, nc, tie_body, jnp.zeros((tq, 1), F32))

    def run_group(pairs):
        soft = [p for p in pairs if p not in STICK_PAIRS]
        slot = {p: n for n, p in enumerate(soft)}
        for p in soft:
            mx_ref[slot[p]] = jnp.full((R, ck), -3e38, F32)
            ls_ref[slot[p]] = jnp.zeros((R, ck), F32)

        def main_body(idx, carry):
            c = nc - 1 - idx
            r0 = pl.multiple_of(c * ck, ck)
            kpos = c * ck + colv
            cz2 = both_rows(jnp.where(kpos <= qpos, 0.0, NEG_BIG))
            strict2 = both_rows(kpos < qpos)
            for p in pairs:
                lg = lax.dot_general(qst_ref[p], k_ref[pl.ds(r0, ck), p * LANES:(p + 1) * LANES], NT_DIMS,
                                     preferred_element_type=F32)
                if p in STICK_PAIRS:
                    z = lg
                    sp = jnp.maximum(z, 0.0) + jnp.log(1.0 + jnp.exp(-jnp.abs(z)))
                    log_keep = jnp.where(strict2, -sp, 0.0)
                    cr = carry_ref[p - 4]
                    later = _split_dot(log_keep, ust_ref[...]) + both_halves(cr)
                    carry_ref[p - 4] = cr + jnp.sum(log_keep, axis=1, keepdims=True)
                    w = jnp.where(strict2, jnp.exp(z - sp + later), 0.0)
                    acc_ref[p] += jnp.dot(w.astype(BF16), v_ref[pl.ds(r0, ck), p * LANES:(p + 1) * LANES],
                                          preferred_element_type=F32)
                    continue
                if p < 2:
                    lg = lg + both_rows(sb_ref[c])
                elif p < 4:
                    col = jnp.sum(jnp.where(lane_r == c, cbias_ref[p - 2], 0.0), axis=1, keepdims=True)
                    lg = lg + col + cz2
                else:
                    pd = p - 6
                    ckr = ck_ref[c]
                    ck_rows = jnp.concatenate([jnp.broadcast_to(ckr[2 * pd:2 * pd + 1, :], (tq, ck)),
                                               jnp.broadcast_to(ckr[2 * pd + 1:2 * pd + 2, :], (tq, ck))], axis=0)
                    lg = lg + (both_halves(cqb_ref[pd]) - ck_rows) + cz2
                lg_ref[slot[p], c] = lg
                mx_ref[slot[p]] = jnp.maximum(mx_ref[slot[p]], lg)
            return carry

        lax.fori_loop(0, nc, main_body, 0)

        for p in soft:
            mb_ref[slot[p]] = jnp.broadcast_to(jnp.max(mx_ref[slot[p]], axis=1, keepdims=True), (R, LANES))

        def pv_body(c, carry):
            r0 = pl.multiple_of(c * ck, ck)
            for p in soft:
                pr = jnp.exp(lg_ref[slot[p], c] - both_halves(mb_ref[slot[p]]))
                ls_ref[slot[p]] += pr
                acc_ref[p] += jnp.dot(pr.astype(BF16), v_ref[pl.ds(r0, ck), p * LANES:(p + 1) * LANES],
                                      preferred_element_type=F32)
            return carry

        lax.fori_loop(0, nc, pv_body, 0)

        for p in pairs:
            a = acc_ref[p]
            if p in slot:
                a = a * (1.0 / jnp.sum(ls_ref[slot[p]], axis=1, keepdims=True))
            o_ref[:, p * LANES:(p + 1) * LANES] = jnp.where(lo_half, a[:tq], a[tq:]).astype(o_ref.dtype)

    for pairs in PAIR_GROUPS:
        run_group(pairs)


def _pattn_call(q, qi, misc, kb, vb, kib, ck4, kbar, tri, ust, *, B, S):
    tq = ATTN_TQ
    ck = ATTN_CK
    nq = S // tq
    nck = S // ck
    T = B * S
    R = 2 * tq
    n_soft = SOFTMAX_SLOTS
    qrow = lambda w: pl.BlockSpec((tq, w), lambda b, i: (b * nq + i, 0))
    seq = lambda w: pl.BlockSpec((S, w), lambda b, i: (b, 0), pipeline_mode=pl.Buffered(1))
    const = lambda a: pl.BlockSpec(a.shape, lambda b, i: (0,) * a.ndim)
    return pl.pallas_call(
        functools.partial(_pattn_kernel, tq=tq, S=S),
        grid=(B, nq),
        in_specs=[qrow(MIX_WIDTH), qrow(IDX_HEADS * IDX_DIM), qrow(LANES), seq(MIX_WIDTH), seq(MIX_WIDTH),
                  seq(LANES), pl.BlockSpec((None, nck, SUBLANES, ck), lambda b, i: (b, 0, 0, 0)),
                  pl.BlockSpec((None, LANES, MIX_WIDTH), lambda b, i: (b, 0, 0)), const(tri), const(ust)],
        out_specs=qrow(MIX_WIDTH),
        out_shape=jax.ShapeDtypeStruct((T, MIX_WIDTH), BF16),
        scratch_shapes=[pltpu.VMEM((N_PAIRS, R, LANES), BF16),
                        pltpu.VMEM((IDX_HEADS * tq, LANES), BF16),
                        pltpu.VMEM((nck, tq, ck), jnp.int32),
                        pltpu.VMEM((nck, tq, ck), F32),
                        pltpu.VMEM((n_soft, nck, R, ck), F32),
                        pltpu.VMEM((n_soft, R, ck), F32),
                        pltpu.VMEM((n_soft, R, ck), F32),
                        pltpu.VMEM((N_PAIRS, R, LANES), F32),
                        pltpu.VMEM((2, R, LANES), F32),
                        pltpu.VMEM((n_soft, R, LANES), F32),
                        pltpu.VMEM((2, R, LANES), F32),
                        pltpu.VMEM((2, R, LANES), F32)],
        compiler_params=pltpu.CompilerParams(dimension_semantics=("arbitrary", "arbitrary"),
                                             vmem_limit_bytes=VMEM_LIMIT),
        name="prompt_attn",
    )(q, qi, misc, kb, vb, kib, ck4, kbar, tri, ust)


def _merge_kernel(x_ref, o_ref, gpre_ref, wg_ref, wbr_ref, wout_ref, gpost_ref, y_ref):
    x = x_ref[...]
    hb = _rms(x, gpre_ref[...]).astype(BF16)
    mix = None
    for m in range(N_MIXERS):
        gl = jnp.dot(hb, wg_ref[:, m * D_MODEL:(m + 1) * D_MODEL], preferred_element_type=F32)
        br = jnp.dot(o_ref[:, m * BRANCH_WIDTH:(m + 1) * BRANCH_WIDTH].astype(BF16), wbr_ref[m],
                     preferred_element_type=F32)
        term = jax.nn.sigmoid(gl) * br
        mix = term if mix is None else mix + term
    out = jnp.dot(mix.astype(BF16), wout_ref[...], preferred_element_type=F32)
    y_ref[...] = x + _rms(out, gpost_ref[...])


def _merge_call(x, o, gpre, wg, wbr, wout, gpost):
    T = x.shape[0]
    tm = MERGE_TM
    row = lambda w: pl.BlockSpec((tm, w), lambda i: (i, 0))
    const = lambda a: pl.BlockSpec(a.shape, lambda i: (0,) * a.ndim)
    return pl.pallas_call(
        _merge_kernel,
        grid=(T // tm,),
        in_specs=[row(D_MODEL), row(MIX_WIDTH), const(gpre), const(wg), const(wbr), const(wout), const(gpost)],
        out_specs=row(D_MODEL),
        out_shape=jax.ShapeDtypeStruct((T, D_MODEL), F32),
        compiler_params=pltpu.CompilerParams(dimension_semantics=("arbitrary",),
                                             vmem_limit_bytes=VMEM_LIMIT),
        name="merge",
    )(x, o, gpre, wg, wbr, wout, gpost)


def _ffn_kernel(*refs, tm, tiles_per_seq, has_state):
    if has_state:
        (x_ref, gpre_ref, wa_ref, wb_ref, wo_ref, cw_ref, cb_ref, gpost_ref, e1_ref, e2_ref,
         y_ref, a_ref, hb_ref, acc_ref) = refs
    else:
        (x_ref, gpre_ref, wa_ref, wb_ref, wo_ref, cw_ref, cb_ref, gpost_ref,
         y_ref, a_ref, hb_ref, acc_ref, prev_ref) = refs
    i = pl.program_id(0)
    f = pl.program_id(1)
    nf = pl.num_programs(1)

    @pl.when(f == 0)
    def _():
        hb_ref[...] = _rms(x_ref[...], gpre_ref[...]).astype(BF16)
        acc_ref[...] = jnp.zeros_like(acc_ref)

    hb = hb_ref[...]
    a = jnp.dot(hb, wa_ref[...], preferred_element_type=F32)
    b = jnp.dot(hb, wb_ref[...], preferred_element_type=F32)
    rowi = lax.broadcasted_iota(jnp.int32, a.shape, 0)
    r1 = pltpu.roll(a, 1, 0)
    r2 = pltpu.roll(a, 2, 0)
    if has_state:
        t = rowi & (SUBLANES - 1)
        p1 = jnp.where(t >= 1, r1, e1_ref[...])
        p2 = jnp.where(t >= 2, r2, e2_ref[...])
        a_ref[...] = a
    else:
        @pl.when(i % tiles_per_seq == 0)
        def _():
            prev_ref[f] = jnp.zeros(prev_ref.shape[1:], F32)
        prev = prev_ref[f]
        p1 = jnp.where(rowi >= 1, r1, prev[SUBLANES - 1:SUBLANES, :])
        p2 = jnp.where(rowi >= 2, r2, jnp.where(rowi == 0, prev[SUBLANES - 2:SUBLANES - 1, :],
                                                 prev[SUBLANES - 1:SUBLANES, :]))
        last = a[tm - SUBLANES:, :]
        prev_ref[f] = last
        a_ref[...] = last
    c = cb_ref[...] + p2 * cw_ref[0:1, :] + p1 * cw_ref[1:2, :] + a * cw_ref[2:3, :]
    gated = (jax.nn.gelu(c, approximate=True) * b).astype(BF16)
    acc_ref[...] += jnp.dot(gated, wo_ref[...], preferred_element_type=F32)

    @pl.when(f == nf - 1)
    def _():
        y_ref[...] = x_ref[...] + _rms(acc_ref[...], gpost_ref[...])


def _ffn_call(x, gpre, wa, wb, wo, cw, cb, gpost, *, seq_len, state=None):
    T = x.shape[0]
    tm = min(FFN_TM, T)
    tf = FFN_TF
    nf = D_FF // tf
    has_state = state is not None
    tiles_per_seq = max(seq_len // tm, 1)
    in_specs = [pl.BlockSpec((tm, D_MODEL), lambda i, f: (i, 0)),
                pl.BlockSpec((1, D_MODEL), lambda i, f: (0, 0)),
                pl.BlockSpec((D_MODEL, tf), lambda i, f: (0, f)),
                pl.BlockSpec((D_MODEL, tf), lambda i, f: (0, f)),
                pl.BlockSpec((tf, D_MODEL), lambda i, f: (f, 0)),
                pl.BlockSpec((SUBLANES, tf), lambda i, f: (0, f)),
                pl.BlockSpec((1, tf), lambda i, f: (0, f)),
                pl.BlockSpec((1, D_MODEL), lambda i, f: (0, 0))]
    args = [x, gpre, wa, wb, wo, cw, cb, gpost]
    scratch = [pltpu.VMEM((tm, D_MODEL), BF16), pltpu.VMEM((tm, D_MODEL), F32)]
    if has_state:
        in_specs += [pl.BlockSpec((tm, tf), lambda i, f: (i, f))] * 2
        args += list(state)
        a_shape = jax.ShapeDtypeStruct((T, D_FF), F32)
        a_spec = pl.BlockSpec((tm, tf), lambda i, f: (i, f))
    else:
        a_shape = jax.ShapeDtypeStruct((T // tm, SUBLANES, D_FF), F32)
        a_spec = pl.BlockSpec((None, SUBLANES, tf), lambda i, f: (i, 0, f))
        scratch.append(pltpu.VMEM((nf, SUBLANES, tf), F32))
    return pl.pallas_call(
        functools.partial(_ffn_kernel, tm=tm, tiles_per_seq=tiles_per_seq, has_state=has_state),
        grid=(T // tm, nf),
        in_specs=in_specs,
        out_specs=(pl.BlockSpec((tm, D_MODEL), lambda i, f: (i, 0)), a_spec),
        out_shape=(jax.ShapeDtypeStruct((T, D_MODEL), F32), a_shape),
        scratch_shapes=scratch,
        compiler_params=pltpu.CompilerParams(dimension_semantics=("arbitrary", "arbitrary"),
                                             vmem_limit_bytes=VMEM_LIMIT),
        name="ffn",
    )(*args)


def _sidx_kernel(pt_ref, qi_ref, misc_ref, kin_ref, tri_ref, hp_ref, ls_ref, kidx_hbm, logf_hbm,
                 sel_ref, ckp_ref, cqb_ref, ckn_ref, kbuf, lbuf, key_ref, sem, *, n_pages, page, lp, pool_base):
    b = pl.program_id(0)
    n_dec = pl.num_programs(0)
    past = n_pages * page
    tq = SUBLANES
    slot = b % 2

    def copies(seq, slot_, p):
        pg = pt_ref[seq, p] + pool_base
        return (pltpu.make_async_copy(kidx_hbm.at[pg], kbuf.at[slot_, :, pl.ds(p * page, page)], sem.at[slot_, 0]),
                pltpu.make_async_copy(logf_hbm.at[pl.ds(pg, 1)], lbuf.at[slot_, pl.ds(p, 1)], sem.at[slot_, 1]))

    def start_all(seq, slot_):
        for p in range(n_pages):
            for cp in copies(seq, slot_, p):
                cp.start()

    @pl.when(b == 0)
    def _():
        start_all(0, 0)

    @pl.when(b + 1 < n_dec)
    def _():
        start_all(b + 1, 1 - slot)

    for p in range(n_pages):
        for cp in copies(b, slot, p):
            cp.wait()

    qi = qi_ref[...]
    qall = jnp.concatenate([qi[:, h * IDX_DIM:(h + 1) * IDX_DIM] for h in range(IDX_HEADS)], axis=0).astype(BF16)
    rel = jnp.maximum(jnp.dot(qall, kbuf[slot].astype(BF16), preferred_element_type=F32), 0.0)
    knew = jnp.concatenate([kin_ref[:, :IDX_DIM], jnp.zeros((LANES - tq, IDX_DIM), F32)], axis=0).astype(BF16)
    reln = jnp.maximum(lax.dot_general(qall, knew, NT_DIMS, preferred_element_type=F32), 0.0)
    misc = misc_ref[...]
    sp = jnp.zeros((tq, past), F32)
    sn = jnp.zeros((tq, LANES), F32)
    for h in range(IDX_HEADS):
        sp = sp + misc[:, h:h + 1] * rel[h * tq:(h + 1) * tq, :]
        sn = sn + misc[:, h:h + 1] * reln[h * tq:(h + 1) * tq, :]
    qrow = lax.broadcasted_iota(jnp.int32, (tq, LANES), 0)
    ncol = lax.broadcasted_iota(jnp.int32, (tq, LANES), 1)
    key_ref[:, :past] = _sortable_key(sp)
    key_ref[:, past:past + LANES] = jnp.where(ncol <= qrow, _sortable_key(sn), INT_MIN)
    if lp > past + LANES:
        key_ref[:, past + LANES:] = jnp.full((tq, lp - past - LANES), INT_MIN, jnp.int32)
    sel = _topk_select(key_ref, tri_ref, DSA_TOPK, tq, lp)
    col = lax.broadcasted_iota(jnp.int32, (tq, lp), 1)
    qr = lax.broadcasted_iota(jnp.int32, (tq, lp), 0)
    sel_ref[...] = jnp.where(jnp.logical_and(sel, col <= past + qr), 1.0, 0.0)

    lg = lbuf[slot]
    within = _split_dot(lg, hp_ref[...])
    totals = jnp.concatenate(
        [jnp.broadcast_to(within[:, (h + 1) * page - 1:(h + 1) * page], (n_pages, page)) for h in range(4)], axis=1)
    offs = _split_dot_left(ls_ref[...], totals)
    ckp = within + offs
    ckp_ref[...] = ckp
    tot = ckp[n_pages - 1:n_pages, :]
    cq_rows = []
    ckn_rows = []
    mt = jnp.concatenate([misc, jnp.zeros((LANES - tq, LANES), F32)], axis=0).T
    for h in range(4):
        tot_h = tot[:, (h + 1) * page - 1:(h + 1) * page]
        cq_rows.append(jnp.broadcast_to(misc[:, 8 + h:9 + h] + tot_h, (tq, LANES)))
        ckn_rows.append(mt[8 + h:9 + h, :] + tot_h)
    cqb_ref[...] = jnp.concatenate(cq_rows, axis=0)
    ckn_ref[...] = jnp.concatenate(ckn_rows + [jnp.zeros((SUBLANES - 4, LANES), F32)], axis=0)


def _sidx_call(page_table, qi, misc, kif, tri, hp, ls, kidx_pool_t, logf_pool, *, lp, pool_base):
    n_dec, n_pages = page_table.shape
    page = kidx_pool_t.shape[2]
    tq = SUBLANES
    blk = lambda w: pl.BlockSpec((tq, w), lambda b, pt: (b, 0))
    const = lambda a: pl.BlockSpec(a.shape, lambda b, pt: (0,) * a.ndim)
    grid_spec = pltpu.PrefetchScalarGridSpec(
        num_scalar_prefetch=1,
        grid=(n_dec,),
        in_specs=[blk(IDX_HEADS * IDX_DIM), blk(LANES), blk(LANES), const(tri), const(hp), const(ls),
                  pl.BlockSpec(memory_space=pl.ANY), pl.BlockSpec(memory_space=pl.ANY)],
        out_specs=(pl.BlockSpec((None, tq, lp), lambda b, pt: (b, 0, 0)),
                   pl.BlockSpec((None, n_pages, 4 * page), lambda b, pt: (b, 0, 0)),
                   pl.BlockSpec((None, 4 * tq, LANES), lambda b, pt: (b, 0, 0)),
                   pl.BlockSpec((None, SUBLANES, LANES), lambda b, pt: (b, 0, 0))),
        scratch_shapes=[pltpu.VMEM((2, IDX_DIM, n_pages * page), F32),
                        pltpu.VMEM((2, n_pages, 4 * page), F32),
                        pltpu.VMEM((tq, lp), jnp.int32),
                        pltpu.SemaphoreType.DMA((2, 2))],
    )
    return pl.pallas_call(
        functools.partial(_sidx_kernel, n_pages=n_pages, page=page, lp=lp, pool_base=pool_base),
        grid_spec=grid_spec,
        out_shape=(jax.ShapeDtypeStruct((n_dec, tq, lp), F32),
                   jax.ShapeDtypeStruct((n_dec, n_pages, 4 * page), F32),
                   jax.ShapeDtypeStruct((n_dec, 4 * tq, LANES), F32),
                   jax.ShapeDtypeStruct((n_dec, SUBLANES, LANES), F32)),
        compiler_params=pltpu.CompilerParams(dimension_semantics=("arbitrary",),
                                             vmem_limit_bytes=VMEM_LIMIT),
        name="sample_index",
    )(page_table, qi, misc, kif, tri, hp, ls, kidx_pool_t, logf_pool)


SATTN_BLOCKS_PER_STEP = 4


def _sattn_kernel(pt_ref, q_ref, kn_ref, vn_ref, selp_ref, seln_ref, ckp_ref, cqb_ref, ckn_ref, ust_ref,
                  *rest, nblk, page):
    npg = SATTN_BLOCKS_PER_STEP * (MOBA_BLOCK // page)
    k_refs = rest[:npg]
    v_refs = rest[npg:2 * npg]
    o_ref, qs_ref, ml_ref, acc_ref, carry_ref, bst_ref, bacc_ref = rest[2 * npg:]
    j = pl.program_id(1)
    nsteps = pl.num_programs(1)
    R = 4 * SUBLANES
    W = BRANCH_WIDTH
    rowi = lax.broadcasted_iota(jnp.int32, (R, W), 0)
    lanei = lax.broadcasted_iota(jnp.int32, (R, W), 1)
    own = (rowi // SUBLANES) == (lanei // HEAD_DIM)
    li = lax.broadcasted_iota(jnp.int32, (R, LANES), 1)

    def online(m_idx, lg, pv):
        m_old = ml_ref[m_idx, :, 0:1]
        l_old = ml_ref[m_idx, :, 1:2]
        m_new = jnp.maximum(m_old, jnp.max(lg, axis=1, keepdims=True))
        alpha = jnp.exp(m_old - m_new)
        p = jnp.exp(lg - m_new)
        ml_ref[m_idx, :, 0:1] = m_new
        ml_ref[m_idx, :, 1:2] = alpha * l_old + jnp.sum(p, axis=1, keepdims=True)
        acc_ref[m_idx] = alpha * acc_ref[m_idx] + pv(p.astype(BF16))

    def block_partial(n, lg, pv, score):
        m_n = jnp.max(lg, axis=1, keepdims=True)
        p = jnp.exp(lg - m_n)
        l_n = jnp.sum(p, axis=1, keepdims=True)
        here = li == n
        bst_ref[0] = jnp.where(here, m_n, bst_ref[0])
        bst_ref[1] = jnp.where(here, l_n, bst_ref[1])
        bst_ref[2] = jnp.where(here, score, bst_ref[2])
        bacc_ref[n] = pv(p.astype(BF16))

    def stick(z, strict, pv, ust):
        sp = jnp.maximum(z, 0.0) + jnp.log(1.0 + jnp.exp(-jnp.abs(z)))
        log_keep = -sp
        if strict is not None:
            log_keep = jnp.where(strict, log_keep, 0.0)
        carry = carry_ref[...]
        later = _split_dot(log_keep, ust) + carry[:, 0:1]
        carry_ref[...] = carry + jnp.sum(log_keep, axis=1, keepdims=True)
        w = jnp.exp(z - sp + later)
        if strict is not None:
            w = jnp.where(strict, w, 0.0)
        acc_ref[2] = acc_ref[2] + pv(w.astype(BF16))

    def rows4(x8):
        return jnp.concatenate([x8] * 4, axis=0)

    @pl.when(j == 0)
    def _():
        q = q_ref[...]
        for m in range(N_MIXERS):
            qm = rows4(q[:, m * W:(m + 1) * W])
            qs_ref[m] = jnp.where(own, qm, 0.0).astype(BF16)
        li3 = lax.broadcasted_iota(jnp.int32, (N_MIXERS, R, LANES), 2)
        ml_ref[...] = jnp.where(li3 == 0, NEG_BIG, 0.0)
        acc_ref[...] = jnp.zeros_like(acc_ref)
        carry_ref[...] = jnp.zeros_like(carry_ref)
        bst_ref[...] = jnp.zeros_like(bst_ref)
        pad = jnp.zeros((LANES - SUBLANES, MIX_WIDTH), F32)
        kn = jnp.concatenate([kn_ref[...], pad], axis=0).astype(BF16)
        vn = jnp.concatenate([vn_ref[...], pad], axis=0).astype(BF16)
        kcol = lax.broadcasted_iota(jnp.int32, (R, LANES), 1)
        qq = lax.broadcasted_iota(jnp.int32, (R, LANES), 0) & (SUBLANES - 1)
        causal = kcol <= qq
        strict = kcol < qq
        lg = [lax.dot_general(qs_ref[m], kn[:, m * W:(m + 1) * W], NT_DIMS, preferred_element_type=F32)
              for m in range(N_MIXERS)]
        pvn = lambda m: (lambda p: jnp.dot(p, vn[:, m * W:(m + 1) * W], preferred_element_type=F32))
        sel = rows4(seln_ref[...]) > 0.5
        online(0, jnp.where(jnp.logical_and(sel, causal), lg[0], -jnp.inf), pvn(0))
        block_partial(nblk, jnp.where(causal, lg[1], NEG_BIG), pvn(1), jnp.zeros((R, 1), F32))
        stick(lg[2], strict, pvn(2), ust_ref[:LANES, :LANES])
        ckn = ckn_ref[...]
        bias = cqb_ref[...] - jnp.concatenate(
            [jnp.broadcast_to(ckn[h:h + 1, :], (SUBLANES, LANES)) for h in range(4)], axis=0)
        online(3, jnp.where(causal, lg[3] + bias, -jnp.inf), pvn(3))

    ppb = MOBA_BLOCK // page
    nb = SATTN_BLOCKS_PER_STEP

    def kt(t, m):
        return jnp.concatenate([r[m * W:(m + 1) * W, :] for r in k_refs[t * ppb:(t + 1) * ppb]],
                               axis=1).astype(BF16)

    def pv(t, m):
        vt = jnp.concatenate([r[m * W:(m + 1) * W, :] for r in v_refs[t * ppb:(t + 1) * ppb]],
                             axis=1).astype(BF16)
        return lambda p: lax.dot_general(p, vt, NT_DIMS, preferred_element_type=F32)

    def pv_step(m):
        def f(p):
            out = None
            for t in range(nb):
                d = pv(t, m)(p[:, t * MOBA_BLOCK:(t + 1) * MOBA_BLOCK])
                out = d if out is None else out + d
            return out
        return f

    lg = [[jnp.dot(qs_ref[m], kt(t, m), preferred_element_type=F32) for m in range(N_MIXERS)] for t in range(nb)]
    step_cols = lambda m: jnp.concatenate([lg[t][m] for t in range(nb)], axis=1)

    online(0, jnp.where(rows4(selp_ref[...]) > 0.5, step_cols(0), -jnp.inf), pv_step(0))
    ckp = ckp_ref[...]
    ck_rows = [jnp.broadcast_to(jnp.concatenate([ckp[tp:tp + 1, h * page:(h + 1) * page]
                                                 for tp in range(nb * ppb)], axis=1), (SUBLANES, nb * MOBA_BLOCK))
               for h in range(4)]
    online(3, step_cols(3) + (cqb_ref[:, 0:1] - jnp.concatenate(ck_rows, axis=0)), pv_step(3))

    for t in range(nb):
        block_partial(nb * (nsteps - 1 - j) + t, lg[t][1], pv(t, 1),
                      jnp.sum(lg[t][1], axis=1, keepdims=True) * (1.0 / MOBA_BLOCK))

    ones = jnp.ones((MOBA_BLOCK, LANES), BF16)
    carry = carry_ref[...]
    sps = [jnp.maximum(lg[t][2], 0.0) + jnp.log(1.0 + jnp.exp(-jnp.abs(lg[t][2]))) for t in range(nb)]
    acc_c = acc_ref[2]
    for t in reversed(range(nb)):
        log_keep = -sps[t]
        hi = log_keep.astype(BF16)
        lo = (log_keep - hi.astype(F32)).astype(BF16)
        later = (jnp.dot(hi, ust_ref[...], preferred_element_type=F32)
                 + jnp.dot(lo, ust_ref[...], preferred_element_type=F32)) + carry[:, 0:1]
        block_sum = jnp.dot(hi, ones, preferred_element_type=F32) + jnp.dot(lo, ones, preferred_element_type=F32)
        acc_c = acc_c + pv(t, 2)(jnp.exp(lg[t][2] - sps[t] + later).astype(BF16))
        carry = carry + block_sum
    acc_ref[2] = acc_c
    carry_ref[...] = carry

    @pl.when(j == nsteps - 1)
    def _():
        m_all = bst_ref[0]
        l_all = bst_ref[1]
        sc_all = bst_ref[2]
        beaten = jnp.zeros((R, LANES), F32)
        for m_ in range(nblk):
            bm = sc_all[:, m_:m_ + 1]
            wins = jnp.logical_or(bm > sc_all, jnp.logical_and(bm == sc_all, m_ < li))
            beaten = beaten + jnp.where(wins, 1.0, 0.0)
        chosen = jnp.logical_or(jnp.logical_and(li < nblk, beaten < float(MOBA_TOPK)), li == nblk)
        m_tot = jnp.max(jnp.where(chosen, m_all, NEG_BIG), axis=1, keepdims=True)
        wgt = jnp.where(chosen, jnp.exp(m_all - m_tot), 0.0)
        l_tot = jnp.sum(wgt * l_all, axis=1, keepdims=True)
        acc_b = jnp.zeros((R, W), F32)
        for n_ in range(nblk + 1):
            acc_b = acc_b + wgt[:, n_:n_ + 1] * bacc_ref[n_]
        res = [acc_ref[0] * (1.0 / ml_ref[0, :, 1:2]), acc_b * (1.0 / l_tot), acc_ref[2],
               acc_ref[3] * (1.0 / ml_ref[3, :, 1:2])]
        for m in range(N_MIXERS):
            r = jnp.where(own, res[m], 0.0)
            o_ref[:, m * W:(m + 1) * W] = (r[0:SUBLANES] + r[SUBLANES:2 * SUBLANES]
                                           + r[2 * SUBLANES:3 * SUBLANES] + r[3 * SUBLANES:])


def _sattn_call(page_table, q, kn, vn, sel, ckp, cqb, ckn, ust, k_pool_t, v_pool_t, *, lp, pool_base):
    n_dec, n_pages = page_table.shape
    page = k_pool_t.shape[2]
    nblk = n_pages * page // MOBA_BLOCK
    ppb = MOBA_BLOCK // page
    pps = ppb * SATTN_BLOCKS_PER_STEP
    nsteps = n_pages // pps
    assert nsteps * pps == n_pages
    past = n_pages * page
    tq = SUBLANES
    R = 4 * tq
    wstep = SATTN_BLOCKS_PER_STEP * MOBA_BLOCK
    blk = lambda w: pl.BlockSpec((tq, w), lambda b, j, pt: (b, 0))
    pg = lambda t: pl.BlockSpec((None, MIX_WIDTH, page),
                                lambda b, j, pt: (pt[b, pps * (nsteps - 1 - j) + t] + pool_base, 0, 0))
    ckp4 = ckp.reshape(n_dec, nsteps, pps, 4 * page)
    grid_spec = pltpu.PrefetchScalarGridSpec(
        num_scalar_prefetch=1,
        grid=(n_dec, nsteps),
        in_specs=[blk(MIX_WIDTH), blk(MIX_WIDTH), blk(MIX_WIDTH),
                  pl.BlockSpec((None, tq, wstep), lambda b, j, pt: (b, 0, nsteps - 1 - j)),
                  pl.BlockSpec((None, tq, LANES), lambda b, j, pt: (b, 0, past // LANES)),
                  pl.BlockSpec((None, None, pps, 4 * page), lambda b, j, pt: (b, nsteps - 1 - j, 0, 0)),
                  pl.BlockSpec((None, R, LANES), lambda b, j, pt: (b, 0, 0)),
                  pl.BlockSpec((None, SUBLANES, LANES), lambda b, j, pt: (b, 0, 0)),
                  pl.BlockSpec(ust.shape, lambda b, j, pt: (0, 0))]
                 + [pg(t) for t in range(pps)] * 2,
        out_specs=blk(MIX_WIDTH),
        scratch_shapes=[pltpu.VMEM((N_MIXERS, R, BRANCH_WIDTH), BF16),
                        pltpu.VMEM((N_MIXERS, R, LANES), F32),
                        pltpu.VMEM((N_MIXERS, R, BRANCH_WIDTH), F32),
                        pltpu.VMEM((R, LANES), F32),
                        pltpu.VMEM((3, R, LANES), F32),
                        pltpu.VMEM((nblk + 1, R, BRANCH_WIDTH), F32)],
    )
    return pl.pallas_call(
        functools.partial(_sattn_kernel, nblk=nblk, page=page),
        grid_spec=grid_spec,
        out_shape=jax.ShapeDtypeStruct((n_dec * tq, MIX_WIDTH), F32),
        compiler_params=pltpu.CompilerParams(dimension_semantics=("arbitrary", "arbitrary"),
                                             vmem_limit_bytes=VMEM_LIMIT),
        name="sample_attn",
    )(page_table, q, kn, vn, sel, sel, ckp4, cqb, ckn, ust, *([k_pool_t] * pps), *([v_pool_t] * pps))


def _rope_tables(pos):
    half = HEAD_DIM // 2
    inv = ROPE_THETA ** (-jnp.arange(half, dtype=F32) / half)
    ang = pos.astype(F32)[:, None] * inv[None, :]
    cos = jnp.cos(ang)
    sin = jnp.sin(ang)
    return jnp.tile(cos, (1, LANES // half)), jnp.tile(jnp.concatenate([-sin, sin], axis=1), (1, LANES // HEAD_DIM))


def _tri_consts():
    r = np.arange(MOBA_BLOCK)
    incl = (r[:, None] <= r[None, :]).astype(np.float32)
    strict_later = (r[:, None] > r[None, :]).astype(np.float32)
    return jnp.asarray(incl, BF16), jnp.asarray(strict_later, BF16)


def _row_cumsum_matrix(tm, seq_len):
    r = np.arange(tm)
    m = (r[None, :] <= r[:, None])
    if seq_len < tm:
        m = m & ((r[None, :] // seq_len) == (r[:, None] // seq_len))
    return jnp.asarray(m.astype(np.float32), BF16)


def kernel(x_prompt, x_sample, cache_k, cache_v, cache_kidx, cache_logf, state_conv, page_table,
           w_in, g_kidx, b_forget, w_branch, w_out, w_ffn_in, conv_w, conv_b, w_ffn_out,
           g_pre_mix, g_post_mix, g_pre_ffn, g_post_ffn):
    B, S, _ = x_prompt.shape
    DB, TS, _ = x_sample.shape
    depth = w_in.shape[0]
    n_pool, page = cache_k.shape[1], cache_k.shape[2]
    n_pages = page_table.shape[1]
    past = n_pages * page
    assert TS == SUBLANES and S % MOBA_BLOCK == 0 and page == LANES
    lp = -(-(past + TS) // MOBA_BLOCK) * MOBA_BLOCK

    cos_p, sin_p = _rope_tables(jnp.arange(S, dtype=jnp.int32))
    cos_s, sin_s = _rope_tables(jnp.tile(past + jnp.arange(TS, dtype=jnp.int32), DB))
    tri, ust = _tri_consts()
    ltri_p = _row_cumsum_matrix(PROJ_TM, S)
    ltri_s = _row_cumsum_matrix(PROJ_TM, TS)
    pr = np.arange(4 * page)
    head_prefix = jnp.asarray(((pr[:, None] // page == pr[None, :] // page)
                               & (pr[:, None] <= pr[None, :])).astype(np.float32), BF16)
    pgr = np.arange(n_pages)
    page_strict = jnp.asarray((pgr[None, :] < pgr[:, None]).astype(np.float32), BF16)

    k_pool = cache_k.transpose(0, 1, 3, 4, 2).reshape(depth * n_pool, MIX_WIDTH, page)
    v_pool = cache_v.transpose(0, 1, 3, 4, 2).reshape(depth * n_pool, MIX_WIDTH, page)
    kidx_pool = cache_kidx.transpose(0, 1, 3, 2).reshape(depth * n_pool, IDX_DIM, page)
    logf_pool = cache_logf.astype(F32).transpose(0, 1, 3, 2).reshape(depth * n_pool, 4 * page)

    yp = x_prompt.reshape(B * S, D_MODEL)
    ys = x_sample.reshape(DB * TS, D_MODEL)
    row2 = lambda v: v.reshape(1, -1)
    outs_p = [[] for _ in range(5)]
    outs_s = [[] for _ in range(5)]
    for l in range(depth):
        w = w_in[l]
        o0 = 3 * MIX_WIDTH
        o1 = o0 + IDX_HEADS * IDX_DIM
        o2 = o1 + IDX_DIM
        o3 = o2 + IDX_HEADS
        o4 = o3 + 4
        wqkv = w[:, :o0].astype(BF16)
        wqi = w[:, o0:o1].astype(BF16)
        wsm = jnp.concatenate([w[:, o1:o2], w[:, o1:o2], w[:, o2:o3], w[:, o3:o4], w[:, o3:o4],
                               jnp.zeros((D_MODEL, LANES - 12), F32)], axis=1).astype(BF16)
        wg = w[:, o4:].astype(BF16)
        wbr = w_branch[l].astype(BF16)
        wout = w_out[l].astype(BF16)
        wa = w_ffn_in[l][:, :D_FF].astype(BF16)
        wb = w_ffn_in[l][:, D_FF:].astype(BF16)
        wo = w_ffn_out[l].astype(BF16)
        cw = jnp.concatenate([conv_w[l], jnp.zeros((SUBLANES - CONV_WIDTH, D_FF), F32)], axis=0)
        cb = row2(conv_b[l])
        gk = row2(jnp.concatenate([g_kidx[l], g_kidx[l]]))
        bfp = row2(jnp.concatenate([jnp.zeros((4,), F32), b_forget[l], b_forget[l], jnp.zeros((LANES - 12,), F32)]))
        gpre, gpost = row2(g_pre_mix[l]), row2(g_post_mix[l])
        gpre_f, gpost_f = row2(g_pre_ffn[l]), row2(g_post_ffn[l])

        q, kf, vf, kb, vb, qi, kif, kib, misc, kbar = _proj_call(
            yp, gpre, wqkv, wqi, wsm, cos_p, sin_p, gk, bfp, ltri_p, seq_len=S, q_dtype=BF16)
        ckt = jnp.pad(misc[:, 8:12].reshape(B, S, 4).transpose(0, 2, 1), ((0, 0), (0, SUBLANES - 4), (0, 0)))
        ck4 = ckt.reshape(B, SUBLANES, S // ATTN_CK, ATTN_CK).transpose(0, 2, 1, 3)
        kbar_p = jnp.pad(kbar.reshape(B, S // MOBA_BLOCK, MIX_WIDTH).astype(BF16),
                         ((0, 0), (0, LANES - S // MOBA_BLOCK), (0, 0)))
        o = _pattn_call(q, qi, misc, kb, vb, kib, ck4, kbar_p, tri, ust, B=B, S=S)
        yp = _merge_call(yp, o, gpre, wg, wbr, wout, gpost)
        yp, alast = _ffn_call(yp, gpre_f, wa, wb, wo, cw, cb, gpost_f, seq_len=S)
        outs_p[0].append(kf.reshape(B, S, N_HEADS, HEAD_DIM))
        outs_p[1].append(vf.reshape(B, S, N_HEADS, HEAD_DIM))
        outs_p[2].append(kif[:, :IDX_DIM].reshape(B, S, IDX_DIM))
        outs_p[3].append(misc[:, 4:8].reshape(B, S, 4))
        alast = alast.reshape(B, -1, SUBLANES, D_FF)[:, -1]
        outs_p[4].append(alast[:, SUBLANES - (CONV_WIDTH - 1):, :])

        q, kf, vf, kb, vb, qi, kif, kib, misc, kbar = _proj_call(
            ys, gpre, wqkv, wqi, wsm, cos_s, sin_s, gk, bfp, ltri_s, seq_len=TS, q_dtype=F32)
        sel, ckp, cqb, ckn = _sidx_call(page_table, qi, misc, kif, tri, head_prefix, page_strict,
                                        kidx_pool, logf_pool, lp=lp, pool_base=l * n_pool)
        o = _sattn_call(page_table, q, kf, vf, sel, ckp, cqb, ckn, ust, k_pool, v_pool,
                        lp=lp, pool_base=l * n_pool)
        ys = _merge_call(ys, o, gpre, wg, wbr, wout, gpost)
        st = state_conv[l].astype(F32)
        zeros = jnp.zeros((DB, TS - 2, D_FF), F32)
        e1 = jnp.concatenate([st[:, 1:2], jnp.zeros((DB, TS - 1, D_FF), F32)], axis=1).reshape(DB * TS, D_FF)
        e2 = jnp.concatenate([st, zeros], axis=1).reshape(DB * TS, D_FF)
        ys, a_all = _ffn_call(ys, gpre_f, wa, wb, wo, cw, cb, gpost_f, seq_len=TS, state=(e1, e2))
        outs_s[0].append(kf.reshape(DB, TS, N_HEADS, HEAD_DIM))
        outs_s[1].append(vf.reshape(DB, TS, N_HEADS, HEAD_DIM))
        outs_s[2].append(kif[:, :IDX_DIM].reshape(DB, TS, IDX_DIM))
        outs_s[3].append(misc[:, 4:8].reshape(DB, TS, 4))
        outs_s[4].append(a_all.reshape(DB, TS, D_FF)[:, TS - (CONV_WIDTH - 1):, :])

    return (yp.reshape(B, S, D_MODEL), ys.reshape(DB, TS, D_MODEL),
            *[jnp.stack(v) for v in outs_p], *[jnp.stack(v) for v in outs_s])
```

```python
import functools

import numpy as np
import jax
import jax.numpy as jnp
from jax import lax
from jax.experimental import pallas as pl
from jax.experimental.pallas import tpu as pltpu

D_MODEL = 1024
HEAD_DIM = 64
N_HEADS = 16
MIX_WIDTH = N_HEADS * HEAD_DIM
BRANCH_WIDTH = 256
N_MIXERS = 4
IDX_HEADS = 4
IDX_DIM = 64
DSA_TOPK = 256
MOBA_BLOCK = 256
MOBA_TOPK = 3
ROPE_THETA = 10000.0
D_FF = 4 * D_MODEL
CONV_WIDTH = 3
NORM_EPS = 1e-6

LANES = 128
SUBLANES = 8
VMEM_LIMIT = 56 * 1024 * 1024

F32 = jnp.float32
BF16 = jnp.bfloat16
NT_DIMS = (((1,), (1,)), ((), ()))
NEG_BIG = -1e30
INT_MIN = -2147483648

PROJ_TM = 256
ATTN_TQ = 256
MERGE_TM = 256
FFN_TM = 512
FFN_TF = 1024


def _rms(x, g):
    return x * lax.rsqrt(jnp.mean(x * x, axis=-1, keepdims=True) + NORM_EPS) * g


def _softplus_neg_abs(x):
    return jnp.log1p(jnp.exp(-jnp.abs(x)))


def _split_dot(a, b_bf16):
    hi = a.astype(BF16)
    lo = (a - hi.astype(F32)).astype(BF16)
    return (jnp.dot(hi, b_bf16, preferred_element_type=F32)
            + jnp.dot(lo, b_bf16, preferred_element_type=F32))


def _split_dot_left(a_bf16, b):
    hi = b.astype(BF16)
    lo = (b - hi.astype(F32)).astype(BF16)
    return (jnp.dot(a_bf16, hi, preferred_element_type=F32)
            + jnp.dot(a_bf16, lo, preferred_element_type=F32))


def _sortable_key(score):
    bits = pltpu.bitcast(score + 0.0, jnp.int32)
    return bits ^ ((bits >> 31) & 0x7FFFFFFF)


def _kth_largest_key(key_ref, kth, rows):
    def body(it, lo):
        cand = lo + lax.shift_left(jnp.int32(1), 31 - it)
        cnt = jnp.sum(jnp.where(key_ref[...] >= cand, 1.0, 0.0), axis=1, keepdims=True)
        return jnp.where(cnt >= kth, cand, lo)
    return lax.fori_loop(0, 32, body, jnp.full((rows, 1), INT_MIN, jnp.int32))


def _topk_select(key_ref, tri_ref, kth, rows, width):
    thr = _kth_largest_key(key_ref, float(kth), rows)
    key = key_ref[...]
    gt = key > thr
    tie = key == thr
    need = float(kth) - jnp.sum(jnp.where(gt, 1.0, 0.0), axis=1, keepdims=True)
    off = jnp.zeros((rows, 1), F32)
    cw = tri_ref.shape[0]
    parts = []
    for c in range(width // cw):
        sl = slice(c * cw, (c + 1) * cw)
        tc = jnp.where(tie[:, sl], 1.0, 0.0).astype(BF16)
        within = jnp.dot(tc, tri_ref[...], preferred_element_type=F32)
        parts.append(jnp.logical_or(gt[:, sl], jnp.logical_and(tie[:, sl], within + off <= need)))
        off = off + within[:, cw - 1:cw]
    return jnp.concatenate(parts, axis=1)


def _proj_kernel(x_ref, g_ref, wqkv_ref, wqi_ref, wsm_ref, cos_ref, sin_ref, gk_ref, bf_ref, ltri_ref,
                 q_ref, kf_ref, vf_ref, kb_ref, vb_ref, qi_ref, kif_ref, kib_ref, misc_ref, kbar_ref,
                 carry_ref, *, tm, tiles_per_seq):
    i = pl.program_id(0)
    if tiles_per_seq > 1:
        @pl.when(i % tiles_per_seq == 0)
        def _():
            carry_ref[...] = jnp.zeros_like(carry_ref)
    hb = _rms(x_ref[...], g_ref[...]).astype(BF16)
    cos = cos_ref[...]
    sin = sin_ref[...]
    lane = lax.broadcasted_iota(jnp.int32, (tm, LANES), 1)
    first_half = (lane & (HEAD_DIM - 1)) < HEAD_DIM // 2

    def rope(t):
        rot = jnp.where(first_half, pltpu.roll(t, LANES - HEAD_DIM // 2, 1), pltpu.roll(t, HEAD_DIM // 2, 1))
        return t * cos + rot * sin

    n_rot = 4
    for c in range(MIX_WIDTH // LANES):
        sl = slice(c * LANES, (c + 1) * LANES)
        qc = jnp.dot(hb, wqkv_ref[:, sl], preferred_element_type=F32)
        kc = jnp.dot(hb, wqkv_ref[:, MIX_WIDTH + c * LANES:MIX_WIDTH + (c + 1) * LANES],
                     preferred_element_type=F32)
        vc = jnp.dot(hb, wqkv_ref[:, 2 * MIX_WIDTH + c * LANES:2 * MIX_WIDTH + (c + 1) * LANES],
                     preferred_element_type=F32)
        if c < n_rot:
            qc = rope(qc)
            kc = rope(kc)
        q_ref[:, sl] = (qc * HEAD_DIM ** -0.5).astype(q_ref.dtype)
        kf_ref[:, sl] = kc
        kb_ref[:, sl] = kc.astype(BF16)
        vf_ref[:, sl] = vc
        vb_ref[:, sl] = vc.astype(BF16)
        kbar_ref[:, sl] = jnp.mean(kc, axis=0, keepdims=True)
    for c in range(IDX_HEADS * IDX_DIM // LANES):
        sl = slice(c * LANES, (c + 1) * LANES)
        qic = rope(jnp.dot(hb, wqi_ref[:, sl], preferred_element_type=F32))
        qi_ref[:, sl] = (qic * IDX_DIM ** -0.5).astype(qi_ref.dtype)

    sm = jnp.dot(hb, wsm_ref[...], preferred_element_type=F32)
    ki = sm[:, :LANES]
    ki = rope(_rms(ki, gk_ref[...]))
    kif_ref[...] = ki
    kib_ref[...] = ki.astype(BF16)

    mi = sm[:, LANES:]
    lf = mi + bf_ref[...]
    lf = jnp.minimum(lf, 0.0) - _softplus_neg_abs(lf)
    is_cum = jnp.logical_and(lane >= 8, lane < 12)
    cum = _split_dot_left(ltri_ref[...], jnp.where(is_cum, lf, 0.0))
    if tiles_per_seq > 1:
        cum = cum + carry_ref[0:1, :]
        carry_ref[0:1, :] = cum[tm - 1:tm, :]
    misc = jnp.where(lane < 4, mi * IDX_HEADS ** -0.5,
                     jnp.where(lane < 8, lf, jnp.where(lane < 12, cum, 0.0)))
    misc_ref[...] = misc


def _proj_call(x, g, wqkv, wqi, wsm, cos_t, sin_t, gk, bfp, ltri, *, seq_len, q_dtype):
    T = x.shape[0]
    tm = PROJ_TM
    tiles_per_seq = max(seq_len // tm, 1)
    n_pos_tiles = cos_t.shape[0] // tm
    row = lambda w: pl.BlockSpec((tm, w), lambda i: (i, 0))
    const = lambda a: pl.BlockSpec(a.shape, lambda i: (0,) * a.ndim)
    pos = pl.BlockSpec((tm, LANES), lambda i: (i % n_pos_tiles, 0))
    out_shapes = (
        jax.ShapeDtypeStruct((T, MIX_WIDTH), q_dtype),
        jax.ShapeDtypeStruct((T, MIX_WIDTH), F32),
        jax.ShapeDtypeStruct((T, MIX_WIDTH), F32),
        jax.ShapeDtypeStruct((T, MIX_WIDTH), BF16),
        jax.ShapeDtypeStruct((T, MIX_WIDTH), BF16),
        jax.ShapeDtypeStruct((T, IDX_HEADS * IDX_DIM), q_dtype),
        jax.ShapeDtypeStruct((T, LANES), F32),
        jax.ShapeDtypeStruct((T, LANES), BF16),
        jax.ShapeDtypeStruct((T, LANES), F32),
        jax.ShapeDtypeStruct((T // tm, 1, MIX_WIDTH), F32),
    )
    out_specs = (row(MIX_WIDTH), row(MIX_WIDTH), row(MIX_WIDTH), row(MIX_WIDTH), row(MIX_WIDTH),
                 row(IDX_HEADS * IDX_DIM), row(LANES), row(LANES), row(LANES),
                 pl.BlockSpec((None, 1, MIX_WIDTH), lambda i: (i, 0, 0)))
    return pl.pallas_call(
        functools.partial(_proj_kernel, tm=tm, tiles_per_seq=tiles_per_seq),
        grid=(T // tm,),
        in_specs=[row(D_MODEL), const(g), const(wqkv), const(wqi), const(wsm), pos, pos,
                  const(gk), const(bfp), const(ltri)],
        out_specs=out_specs,
        out_shape=out_shapes,
        scratch_shapes=[pltpu.VMEM((SUBLANES, LANES), F32)],
        compiler_params=pltpu.CompilerParams(dimension_semantics=("arbitrary",),
                                             vmem_limit_bytes=VMEM_LIMIT),
        name="proj",
    )(x, g, wqkv, wqi, wsm, cos_t, sin_t, gk, bfp, ltri)


ATTN_CK = MOBA_BLOCK
N_PAIRS = N_HEADS // 2
STICK_PAIRS = (4, 5)
PAIR_GROUPS = ((0, 1, 2, 3), (4, 5, 6, 7))
SOFTMAX_SLOTS = 4
BISECT_ROWS = 128


def _pattn_kernel(q_ref, qi_ref, misc_ref, k_ref, v_ref, ki_ref, ck_ref, kbar_ref, tri_ref, ust_ref,
                  o_ref,
                  qst_ref, qist_ref, key_ref, sb_ref, lg_ref, mx_ref, ls_ref, acc_ref, cbias_ref, mb_ref,
                  cqb_ref, carry_ref, *, tq, S):
    ck = ATTN_CK
    nck = S // ck
    R = 2 * tq
    i = pl.program_id(1)
    q0 = i * tq
    nc = q0 // ck + 1
    cur = q0 // MOBA_BLOCK
    qpos = q0 + lax.broadcasted_iota(jnp.int32, (tq, 1), 0)
    colv = lax.broadcasted_iota(jnp.int32, (tq, ck), 1)
    lane = lax.broadcasted_iota(jnp.int32, (tq, LANES), 1)
    lo_half = lane < HEAD_DIM
    lane_r = lax.broadcasted_iota(jnp.int32, (R, LANES), 1)
    misc = misc_ref[...]

    def both_rows(x):
        return jnp.concatenate([x, x], axis=0)

    def both_halves(x):
        return jnp.concatenate([x] * (ck // LANES), axis=1)

    for p in range(N_PAIRS):
        blk = q_ref[:, p * LANES:(p + 1) * LANES]
        zero = jnp.zeros_like(blk)
        qst_ref[p, :tq, :] = jnp.where(lo_half, blk, zero)
        qst_ref[p, tq:, :] = jnp.where(lo_half, zero, blk)
    for h in range(IDX_HEADS):
        blk = qi_ref[:, (h // 2) * LANES:(h // 2 + 1) * LANES]
        zero = jnp.zeros_like(blk)
        qist_ref[h * tq:(h + 1) * tq, :] = jnp.where(lo_half, blk, zero) if h % 2 == 0 else jnp.where(lo_half, zero, blk)
    acc_ref[...] = jnp.zeros_like(acc_ref)
    carry_ref[...] = jnp.zeros_like(carry_ref)

    bi8 = lax.broadcasted_iota(jnp.int32, (SUBLANES, R), 0)
    for pb in range(2):
        p = 2 + pb
        sc8 = lax.dot_general(kbar_ref[:SUBLANES, p * LANES:(p + 1) * LANES], qst_ref[p], NT_DIMS,
                              preferred_element_type=F32)
        beaten = jnp.zeros((SUBLANES, R), F32)
        for m_ in range(nck):
            bm = jnp.sum(jnp.where(bi8 == m_, sc8, 0.0), axis=0, keepdims=True)
            wins = jnp.logical_or(bm > sc8, jnp.logical_and(bm == sc8, m_ < bi8))
            beaten = beaten + jnp.where(jnp.logical_and(wins, m_ < cur), 1.0, 0.0)
        chosen = jnp.logical_or(jnp.logical_and(bi8 < cur, beaten < float(MOBA_TOPK)), bi8 == cur)
        cb_t = jnp.concatenate([jnp.where(chosen, 0.0, NEG_BIG),
                                jnp.full((LANES - SUBLANES, R), NEG_BIG, F32)], axis=0)
        cbias_ref[pb] = cb_t.T
    for pd in range(2):
        cqb_ref[pd, :tq, :] = jnp.broadcast_to(misc[:, 8 + 2 * pd:9 + 2 * pd], (tq, LANES))
        cqb_ref[pd, tq:, :] = jnp.broadcast_to(misc[:, 9 + 2 * pd:10 + 2 * pd], (tq, LANES))

    def score_body(c, carry):
        r0 = pl.multiple_of(c * ck, ck)
        rel = jnp.maximum(lax.dot_general(qist_ref[...], ki_ref[pl.ds(r0, ck), :], NT_DIMS,
                                          preferred_element_type=F32), 0.0)
        score = misc[:, 0:1] * rel[0:tq]
        for h in range(1, IDX_HEADS):
            score = score + misc[:, h:h + 1] * rel[h * tq:(h + 1) * tq]
        key_ref[c] = jnp.where(c * ck + colv <= qpos, _sortable_key(score), INT_MIN)
        return carry

    lax.fori_loop(0, nc, score_body, 0)

    rb = BISECT_ROWS

    def search(n_chunks):
        def count(r_lo, pred):
            acc = jnp.zeros((rb, LANES), F32)
            for c in range(n_chunks):
                for g in range(ck // LANES):
                    acc = acc + jnp.where(pred(key_ref[c, r_lo:r_lo + rb, g * LANES:(g + 1) * LANES]), 1.0, 0.0)
            return jnp.sum(acc, axis=1, keepdims=True)

        def run():
            starts = list(range(0, tq, rb))

            def bit_body(it, los):
                bit = lax.shift_left(jnp.int32(1), 31 - it)
                out = []
                for r_lo, lo in zip(starts, los):
                    cand = lo + bit
                    out.append(jnp.where(count(r_lo, lambda k_: k_ >= cand) >= float(DSA_TOPK), cand, lo))
                return tuple(out)

            thrs = lax.fori_loop(0, 32, bit_body,
                                 tuple(jnp.full((rb, 1), INT_MIN, jnp.int32) for _ in starts))
            needs = [float(DSA_TOPK) - count(r_lo, lambda k_: k_ > t) for r_lo, t in zip(starts, thrs)]
            return jnp.concatenate(thrs, axis=0), jnp.concatenate(needs, axis=0)
        return run

    thr, need = lax.switch(nc - 1, [search(n) for n in range(1, nck + 1)])

    def tie_body(c, off):
        key = key_ref[c]
        tie = key == thr
        pre = jnp.dot(jnp.where(tie, 1.0, 0.0).astype(BF16), tri_ref[...], preferred_element_type=F32) + off
        sel = jnp.logical_or(key > thr, jnp.logical_and(tie, pre <= need))
        sel = jnp.logical_and(sel, c * ck + colv <= qpos)
        sb_ref[c] = jnp.where(sel, 0.0, NEG_BIG)
        return pre[:, ck - 1:ck]

    lax.fori_loop(0, nc, tie_body, jnp.zeros((tq, 1), F32))

    def run_group(pairs):
        soft = [p for p in pairs if p not in STICK_PAIRS]
        slot = {p: n for n, p in enumerate(soft)}
        for p in soft:
            mx_ref[slot[p]] = jnp.full((R, ck), -3e38, F32)
            ls_ref[slot[p]] = jnp.zeros((R, ck), F32)

        def main_body(idx, carry):
            c = nc - 1 - idx
            r0 = pl.multiple_of(c * ck, ck)
            kpos = c * ck + colv
            cz2 = both_rows(jnp.where(kpos <= qpos, 0.0, NEG_BIG))
            strict2 = both_rows(kpos < qpos)
            for p in pairs:
                lg = lax.dot_general(qst_ref[p], k_ref[pl.ds(r0, ck), p * LANES:(p + 1) * LANES], NT_DIMS,
                                     preferred_element_type=F32)
                if p in STICK_PAIRS:
                    z = lg
                    sp = jnp.maximum(z, 0.0) + jnp.log(1.0 + jnp.exp(-jnp.abs(z)))
                    log_keep = jnp.where(strict2, -sp, 0.0)
                    cr = carry_ref[p - 4]
                    later = _split_dot(log_keep, ust_ref[...]) + both_halves(cr)
                    carry_ref[p - 4] = cr + jnp.sum(log_keep, axis=1, keepdims=True)
                    w = jnp.where(strict2, jnp.exp(z - sp + later), 0.0)
                    acc_ref[p] += jnp.dot(w.astype(BF16), v_ref[pl.ds(r0, ck), p * LANES:(p + 1) * LANES],
                                          preferred_element_type=F32)
                    continue
                if p < 2:
                    lg = lg + both_rows(sb_ref[c])
                elif p < 4:
                    col = jnp.sum(jnp.where(lane_r == c, cbias_ref[p - 2], 0.0), axis=1, keepdims=True)
                    lg = lg + col + cz2
                else:
                    pd = p - 6
                    ckr = ck_ref[c]
                    ck_rows = jnp.concatenate([jnp.broadcast_to(ckr[2 * pd:2 * pd + 1, :], (tq, ck)),
                                               jnp.broadcast_to(ckr[2 * pd + 1:2 * pd + 2, :], (tq, ck))], axis=0)
                    lg = lg + (both_halves(cqb_ref[pd]) - ck_rows) + cz2
                lg_ref[slot[p], c] = lg
                mx_ref[slot[p]] = jnp.maximum(mx_ref[slot[p]], lg)
            return carry

        lax.fori_loop(0, nc, main_body, 0)

        for p in soft:
            mb_ref[slot[p]] = jnp.broadcast_to(jnp.max(mx_ref[slot[p]], axis=1, keepdims=True), (R, LANES))

        def pv_body(c, carry):
            r0 = pl.multiple_of(c * ck, ck)
            for p in soft:
                pr = jnp.exp(lg_ref[slot[p], c] - both_halves(mb_ref[slot[p]]))
                ls_ref[slot[p]] += pr
                acc_ref[p] += jnp.dot(pr.astype(BF16), v_ref[pl.ds(r0, ck), p * LANES:(p + 1) * LANES],
                                      preferred_element_type=F32)
            return carry

        lax.fori_loop(0, nc, pv_body, 0)

        for p in pairs:
            a = acc_ref[p]
            if p in slot:
                a = a * (1.0 / jnp.sum(ls_ref[slot[p]], axis=1, keepdims=True))
            o_ref[:, p * LANES:(p + 1) * LANES] = jnp.where(lo_half, a[:tq], a[tq:]).astype(o_ref.dtype)

    for pairs in PAIR_GROUPS:
        run_group(pairs)


def _pattn_call(q, qi, misc, kb, vb, kib, ck4, kbar, tri, ust, *, B, S):
    tq = ATTN_TQ
    ck = ATTN_CK
    nq = S // tq
    nck = S // ck
    T = B * S
    R = 2 * tq
    n_soft = SOFTMAX_SLOTS
    qrow = lambda w: pl.BlockSpec((tq, w), lambda b, i: (b * nq + i, 0))
    seq = lambda w: pl.BlockSpec((S, w), lambda b, i: (b, 0), pipeline_mode=pl.Buffered(1))
    const = lambda a: pl.BlockSpec(a.shape, lambda b, i: (0,) * a.ndim)
    return pl.pallas_call(
        functools.partial(_pattn_kernel, tq=tq, S=S),
        grid=(B, nq),
        in_specs=[qrow(MIX_WIDTH), qrow(IDX_HEADS * IDX_DIM), qrow(LANES), seq(MIX_WIDTH), seq(MIX_WIDTH),
                  seq(LANES), pl.BlockSpec((None, nck, SUBLANES, ck), lambda b, i: (b, 0, 0, 0)),
                  pl.BlockSpec((None, LANES, MIX_WIDTH), lambda b, i: (b, 0, 0)), const(tri), const(ust)],
        out_specs=qrow(MIX_WIDTH),
        out_shape=jax.ShapeDtypeStruct((T, MIX_WIDTH), BF16),
        scratch_shapes=[pltpu.VMEM((N_PAIRS, R, LANES), BF16),
                        pltpu.VMEM((IDX_HEADS * tq, LANES), BF16),
                        pltpu.VMEM((nck, tq, ck), jnp.int32),
                        pltpu.VMEM((nck, tq, ck), F32),
                        pltpu.VMEM((n_soft, nck, R, ck), F32),
                        pltpu.VMEM((n_soft, R, ck), F32),
                        pltpu.VMEM((n_soft, R, ck), F32),
                        pltpu.VMEM((N_PAIRS, R, LANES), F32),
                        pltpu.VMEM((2, R, LANES), F32),
                        pltpu.VMEM((n_soft, R, LANES), F32),
                        pltpu.VMEM((2, R, LANES), F32),
                        pltpu.VMEM((2, R, LANES), F32)],
        compiler_params=pltpu.CompilerParams(dimension_semantics=("arbitrary", "arbitrary"),
                                             vmem_limit_bytes=VMEM_LIMIT),
        name="prompt_attn",
    )(q, qi, misc, kb, vb, kib, ck4, kbar, tri, ust)


def _merge_kernel(x_ref, o_ref, gpre_ref, wg_ref, wbr_ref, wout_ref, gpost_ref, y_ref):
    x = x_ref[...]
    hb = _rms(x, gpre_ref[...]).astype(BF16)
    mix = None
    for m in range(N_MIXERS):
        gl = jnp.dot(hb, wg_ref[:, m * D_MODEL:(m + 1) * D_MODEL], preferred_element_type=F32)
        br = jnp.dot(o_ref[:, m * BRANCH_WIDTH:(m + 1) * BRANCH_WIDTH].astype(BF16), wbr_ref[m],
                     preferred_element_type=F32)
        term = jax.nn.sigmoid(gl) * br
        mix = term if mix is None else mix + term
    out = jnp.dot(mix.astype(BF16), wout_ref[...], preferred_element_type=F32)
    y_ref[...] = x + _rms(out, gpost_ref[...])


def _merge_call(x, o, gpre, wg, wbr, wout, gpost):
    T = x.shape[0]
    tm = MERGE_TM
    row = lambda w: pl.BlockSpec((tm, w), lambda i: (i, 0))
    const = lambda a: pl.BlockSpec(a.shape, lambda i: (0,) * a.ndim)
    return pl.pallas_call(
        _merge_kernel,
        grid=(T // tm,),
        in_specs=[row(D_MODEL), row(MIX_WIDTH), const(gpre), const(wg), const(wbr), const(wout), const(gpost)],
        out_specs=row(D_MODEL),
        out_shape=jax.ShapeDtypeStruct((T, D_MODEL), F32),
        compiler_params=pltpu.CompilerParams(dimension_semantics=("arbitrary",),
                                             vmem_limit_bytes=VMEM_LIMIT),
        name="merge",
    )(x, o, gpre, wg, wbr, wout, gpost)


def _ffn_kernel(*refs, tm, tiles_per_seq, has_state):
    if has_state:
        (x_ref, gpre_ref, wa_ref, wb_ref, wo_ref, cw_ref, cb_ref, gpost_ref, e1_ref, e2_ref,
         y_ref, a_ref, hb_ref, acc_ref) = refs
    else:
        (x_ref, gpre_ref, wa_ref, wb_ref, wo_ref, cw_ref, cb_ref, gpost_ref,
         y_ref, a_ref, hb_ref, acc_ref, prev_ref) = refs
    i = pl.program_id(0)
    f = pl.program_id(1)
    nf = pl.num_programs(1)

    @pl.when(f == 0)
    def _():
        hb_ref[...] = _rms(x_ref[...], gpre_ref[...]).astype(BF16)
        acc_ref[...] = jnp.zeros_like(acc_ref)

    if not has_state:
        @pl.when(i % tiles_per_seq == 0)
        def _():
            prev_ref[f] = jnp.zeros(prev_ref.shape[1:], F32)

    hb = hb_ref[...]
    a = jnp.dot(hb, wa_ref[...], preferred_element_type=F32)
    b = jnp.dot(hb, wb_ref[...], preferred_element_type=F32)
    rowi = lax.broadcasted_iota(jnp.int32, a.shape, 0)
    r1 = pltpu.roll(a, 1, 0)
    r2 = pltpu.roll(a, 2, 0)
    if has_state:
        t = rowi & (SUBLANES - 1)
        p1 = jnp.where(t >= 1, r1, e1_ref[...])
        p2 = jnp.where(t >= 2, r2, e2_ref[...])
        a_ref[...] = a
    else:
        prev = prev_ref[f]
        p1 = jnp.where(rowi >= 1, r1, prev[SUBLANES - 1:SUBLANES, :])
        p2 = jnp.where(rowi >= 2, r2, jnp.where(rowi == 0, prev[SUBLANES - 2:SUBLANES - 1, :],
                                                 prev[SUBLANES - 1:SUBLANES, :]))
        last = a[tm - SUBLANES:, :]
        prev_ref[f] = last
        a_ref[...] = last
    c = cb_ref[...] + p2 * cw_ref[0:1, :] + p1 * cw_ref[1:2, :] + a * cw_ref[2:3, :]
    gated = (jax.nn.gelu(c, approximate=True) * b).astype(BF16)
    acc_ref[...] += jnp.dot(gated, wo_ref[...], preferred_element_type=F32)

    @pl.when(f == nf - 1)
    def _():
        y_ref[...] = x_ref[...] + _rms(acc_ref[...], gpost_ref[...])


def _ffn_call(x, gpre, wa, wb, wo, cw, cb, gpost, *, seq_len, state=None):
    T = x.shape[0]
    tm = min(FFN_TM, T)
    tf = FFN_TF
    nf = D_FF // tf
    has_state = state is not None
    tiles_per_seq = max(seq_len // tm, 1)
    in_specs = [pl.BlockSpec((tm, D_MODEL), lambda i, f: (i, 0)),
                pl.BlockSpec((1, D_MODEL), lambda i, f: (0, 0)),
                pl.BlockSpec((D_MODEL, tf), lambda i, f: (0, f)),
                pl.BlockSpec((D_MODEL, tf), lambda i, f: (0, f)),
                pl.BlockSpec((tf, D_MODEL), lambda i, f: (f, 0)),
                pl.BlockSpec((SUBLANES, tf), lambda i, f: (0, f)),
                pl.BlockSpec((1, tf), lambda i, f: (0, f)),
                pl.BlockSpec((1, D_MODEL), lambda i, f: (0, 0))]
    args = [x, gpre, wa, wb, wo, cw, cb, gpost]
    scratch = [pltpu.VMEM((tm, D_MODEL), BF16), pltpu.VMEM((tm, D_MODEL), F32)]
    if has_state:
        in_specs += [pl.BlockSpec((tm, tf), lambda i, f: (i, f))] * 2
        args += list(state)
        a_shape = jax.ShapeDtypeStruct((T, D_FF), F32)
        a_spec = pl.BlockSpec((tm, tf), lambda i, f: (i, f))
    else:
        a_shape = jax.ShapeDtypeStruct((T // tm, SUBLANES, D_FF), F32)
        a_spec = pl.BlockSpec((None, SUBLANES, tf), lambda i, f: (i, 0, f))
        scratch.append(pltpu.VMEM((nf, SUBLANES, tf), F32))
    return pl.pallas_call(
        functools.partial(_ffn_kernel, tm=tm, tiles_per_seq=tiles_per_seq, has_state=has_state),
        grid=(T // tm, nf),
        in_specs=in_specs,
        out_specs=(pl.BlockSpec((tm, D_MODEL), lambda i, f: (i, 0)), a_spec),
        out_shape=(jax.ShapeDtypeStruct((T, D_MODEL), F32), a_shape),
        scratch_shapes=scratch,
        compiler_params=pltpu.CompilerParams(dimension_semantics=("arbitrary", "arbitrary"),
                                             vmem_limit_bytes=VMEM_LIMIT),
        name="ffn",
    )(*args)


SIDX_SEQS = 4


def _sidx_kernel(pt_ref, qi_ref, misc_ref, kin_ref, tri_ref, hp_ref, ls_ref, kidx_hbm, logf_hbm,
                 sel_ref, ckp_ref, cqb_ref, ckn_ref, kbuf, lbuf, key_ref, sem, *, n_pages, page, lp, pool_base):
    G = SIDX_SEQS
    b = pl.program_id(0)
    n_steps = pl.num_programs(0)
    past = n_pages * page
    tq = SUBLANES
    slot = b % 2

    def copies(step, slot_, g, p):
        pg = pt_ref[step * G + g, p] + pool_base
        return (pltpu.make_async_copy(kidx_hbm.at[pg], kbuf.at[slot_, g, :, pl.ds(p * page, page)], sem.at[slot_, 0]),
                pltpu.make_async_copy(logf_hbm.at[pl.ds(pg, 1)], lbuf.at[slot_, g, pl.ds(p, 1)], sem.at[slot_, 1]))

    def start_all(step, slot_):
        for g in range(G):
            for p in range(n_pages):
                for cp in copies(step, slot_, g, p):
                    cp.start()

    @pl.when(b == 0)
    def _():
        start_all(0, 0)

    @pl.when(b + 1 < n_steps)
    def _():
        start_all(b + 1, 1 - slot)

    for g in range(G):
        for p in range(n_pages):
            for cp in copies(b, slot, g, p):
                cp.wait()

    qrow = lax.broadcasted_iota(jnp.int32, (tq, LANES), 0)
    ncol = lax.broadcasted_iota(jnp.int32, (tq, LANES), 1)
    for g in range(G):
        rows = slice(g * tq, (g + 1) * tq)
        qi = qi_ref[rows, :]
        qall = jnp.concatenate([qi[:, h * IDX_DIM:(h + 1) * IDX_DIM] for h in range(IDX_HEADS)], axis=0).astype(BF16)
        rel = jnp.maximum(jnp.dot(qall, kbuf[slot, g].astype(BF16), preferred_element_type=F32), 0.0)
        knew = jnp.concatenate([kin_ref[rows, :IDX_DIM], jnp.zeros((LANES - tq, IDX_DIM), F32)], axis=0).astype(BF16)
        reln = jnp.maximum(lax.dot_general(qall, knew, NT_DIMS, preferred_element_type=F32), 0.0)
        misc = misc_ref[rows, :]
        sp = jnp.zeros((tq, past), F32)
        sn = jnp.zeros((tq, LANES), F32)
        for h in range(IDX_HEADS):
            sp = sp + misc[:, h:h + 1] * rel[h * tq:(h + 1) * tq, :]
            sn = sn + misc[:, h:h + 1] * reln[h * tq:(h + 1) * tq, :]
        key_ref[rows, :past] = _sortable_key(sp)
        key_ref[rows, past:past + LANES] = jnp.where(ncol <= qrow, _sortable_key(sn), INT_MIN)
        if lp > past + LANES:
            key_ref[rows, past + LANES:] = jnp.full((tq, lp - past - LANES), INT_MIN, jnp.int32)

        lg = lbuf[slot, g]
        within = _split_dot(lg, hp_ref[...])
        totals = jnp.concatenate(
            [jnp.broadcast_to(within[:, (h + 1) * page - 1:(h + 1) * page], (n_pages, page)) for h in range(4)], axis=1)
        offs = _split_dot_left(ls_ref[...], totals)
        ckp = within + offs
        ckp_ref[g] = ckp
        tot = ckp[n_pages - 1:n_pages, :]
        cq_rows = []
        ckn_rows = []
        mt = jnp.concatenate([misc, jnp.zeros((LANES - tq, LANES), F32)], axis=0).T
        for h in range(4):
            tot_h = tot[:, (h + 1) * page - 1:(h + 1) * page]
            cq_rows.append(jnp.broadcast_to(misc[:, 8 + h:9 + h] + tot_h, (tq, LANES)))
            ckn_rows.append(mt[8 + h:9 + h, :] + tot_h)
        cqb_ref[g] = jnp.concatenate(cq_rows, axis=0)
        ckn_ref[g] = jnp.concatenate(ckn_rows + [jnp.zeros((SUBLANES - 4, LANES), F32)], axis=0)

    sel = _topk_select(key_ref, tri_ref, DSA_TOPK, G * tq, lp)
    col = lax.broadcasted_iota(jnp.int32, (G * tq, lp), 1)
    qr = lax.broadcasted_iota(jnp.int32, (G * tq, lp), 0) & (tq - 1)
    selv = jnp.where(jnp.logical_and(sel, col <= past + qr), 1.0, 0.0)
    for g in range(G):
        sel_ref[g] = selv[g * tq:(g + 1) * tq, :]


def _sidx_call(page_table, qi, misc, kif, tri, hp, ls, kidx_pool_t, logf_pool, *, lp, pool_base):
    n_dec, n_pages = page_table.shape
    page = kidx_pool_t.shape[2]
    tq = SUBLANES
    G = SIDX_SEQS
    assert n_dec % G == 0
    blk = lambda w: pl.BlockSpec((G * tq, w), lambda b, pt: (b, 0))
    const = lambda a: pl.BlockSpec(a.shape, lambda b, pt: (0,) * a.ndim)
    grid_spec = pltpu.PrefetchScalarGridSpec(
        num_scalar_prefetch=1,
        grid=(n_dec // G,),
        in_specs=[blk(IDX_HEADS * IDX_DIM), blk(LANES), blk(LANES), const(tri), const(hp), const(ls),
                  pl.BlockSpec(memory_space=pl.ANY), pl.BlockSpec(memory_space=pl.ANY)],
        out_specs=(pl.BlockSpec((G, tq, lp), lambda b, pt: (b, 0, 0)),
                   pl.BlockSpec((G, n_pages, 4 * page), lambda b, pt: (b, 0, 0)),
                   pl.BlockSpec((G, 4 * tq, LANES), lambda b, pt: (b, 0, 0)),
                   pl.BlockSpec((G, SUBLANES, LANES), lambda b, pt: (b, 0, 0))),
        scratch_shapes=[pltpu.VMEM((2, G, IDX_DIM, n_pages * page), F32),
                        pltpu.VMEM((2, G, n_pages, 4 * page), F32),
                        pltpu.VMEM((G * tq, lp), jnp.int32),
                        pltpu.SemaphoreType.DMA((2, 2))],
    )
    return pl.pallas_call(
        functools.partial(_sidx_kernel, n_pages=n_pages, page=page, lp=lp, pool_base=pool_base),
        grid_spec=grid_spec,
        out_shape=(jax.ShapeDtypeStruct((n_dec, tq, lp), F32),
                   jax.ShapeDtypeStruct((n_dec, n_pages, 4 * page), F32),
                   jax.ShapeDtypeStruct((n_dec, 4 * tq, LANES), F32),
                   jax.ShapeDtypeStruct((n_dec, SUBLANES, LANES), F32)),
        compiler_params=pltpu.CompilerParams(dimension_semantics=("arbitrary",),
                                             vmem_limit_bytes=VMEM_LIMIT),
        name="sample_index",
    )(page_table, qi, misc, kif, tri, hp, ls, kidx_pool_t, logf_pool)


SATTN_BLOCKS_PER_STEP = 4


def _sattn_kernel(pt_ref, q_ref, kn_ref, vn_ref, selp_ref, seln_ref, ckp_ref, cqb_ref, ckn_ref, ust_ref,
                  *rest, nblk, page):
    npg = SATTN_BLOCKS_PER_STEP * (MOBA_BLOCK // page)
    k_refs = rest[:npg]
    v_refs = rest[npg:2 * npg]
    o_ref, qs_ref, ml_ref, acc_ref, carry_ref, bst_ref, bacc_ref = rest[2 * npg:]
    j = pl.program_id(1)
    nsteps = pl.num_programs(1)
    R = 4 * SUBLANES
    W = BRANCH_WIDTH
    rowi = lax.broadcasted_iota(jnp.int32, (R, W), 0)
    lanei = lax.broadcasted_iota(jnp.int32, (R, W), 1)
    own = (rowi // SUBLANES) == (lanei // HEAD_DIM)
    li = lax.broadcasted_iota(jnp.int32, (R, LANES), 1)

    def online(m_idx, lg, pv):
        m_old = ml_ref[m_idx, :, 0:1]
        l_old = ml_ref[m_idx, :, 1:2]
        m_new = jnp.maximum(m_old, jnp.max(lg, axis=1, keepdims=True))
        alpha = jnp.exp(m_old - m_new)
        p = jnp.exp(lg - m_new)
        ml_ref[m_idx, :, 0:1] = m_new
        ml_ref[m_idx, :, 1:2] = alpha * l_old + jnp.sum(p, axis=1, keepdims=True)
        acc_ref[m_idx] = alpha * acc_ref[m_idx] + pv(p.astype(BF16))

    def block_partial(n, lg, pv, score):
        m_n = jnp.max(lg, axis=1, keepdims=True)
        p = jnp.exp(lg - m_n)
        l_n = jnp.sum(p, axis=1, keepdims=True)
        here = li == n
        bst_ref[0] = jnp.where(here, m_n, bst_ref[0])
        bst_ref[1] = jnp.where(here, l_n, bst_ref[1])
        bst_ref[2] = jnp.where(here, score, bst_ref[2])
        bacc_ref[n] = pv(p.astype(BF16))

    def stick(z, strict, pv, ust):
        sp = jnp.maximum(z, 0.0) + jnp.log(1.0 + jnp.exp(-jnp.abs(z)))
        log_keep = -sp
        if strict is not None:
            log_keep = jnp.where(strict, log_keep, 0.0)
        carry = carry_ref[...]
        later = _split_dot(log_keep, ust) + carry[:, 0:1]
        carry_ref[...] = carry + jnp.sum(log_keep, axis=1, keepdims=True)
        w = jnp.exp(z - sp + later)
        if strict is not None:
            w = jnp.where(strict, w, 0.0)
        acc_ref[2] = acc_ref[2] + pv(w.astype(BF16))

    def rows4(x8):
        return jnp.concatenate([x8] * 4, axis=0)

    @pl.when(j == 0)
    def _():
        q = q_ref[...]
        for m in range(N_MIXERS):
            qm = rows4(q[:, m * W:(m + 1) * W])
            qs_ref[m] = jnp.where(own, qm, 0.0).astype(BF16)
        li3 = lax.broadcasted_iota(jnp.int32, (N_MIXERS, R, LANES), 2)
        ml_ref[...] = jnp.where(li3 == 0, NEG_BIG, 0.0)
        acc_ref[...] = jnp.zeros_like(acc_ref)
        carry_ref[...] = jnp.zeros_like(carry_ref)
        bst_ref[...] = jnp.zeros_like(bst_ref)
        pad = jnp.zeros((LANES - SUBLANES, MIX_WIDTH), F32)
        kn = jnp.concatenate([kn_ref[...], pad], axis=0).astype(BF16)
        vn = jnp.concatenate([vn_ref[...], pad], axis=0).astype(BF16)
        kcol = lax.broadcasted_iota(jnp.int32, (R, LANES), 1)
        qq = lax.broadcasted_iota(jnp.int32, (R, LANES), 0) & (SUBLANES - 1)
        causal = kcol <= qq
        strict = kcol < qq
        lg = [lax.dot_general(qs_ref[m], kn[:, m * W:(m + 1) * W], NT_DIMS, preferred_element_type=F32)
              for m in range(N_MIXERS)]
        pvn = lambda m: (lambda p: jnp.dot(p, vn[:, m * W:(m + 1) * W], preferred_element_type=F32))
        sel = rows4(seln_ref[...]) > 0.5
        online(0, jnp.where(jnp.logical_and(sel, causal), lg[0], -jnp.inf), pvn(0))
        block_partial(nblk, jnp.where(causal, lg[1], NEG_BIG), pvn(1), jnp.zeros((R, 1), F32))
        stick(lg[2], strict, pvn(2), ust_ref[:LANES, :LANES])
        ckn = ckn_ref[...]
        bias = cqb_ref[...] - jnp.concatenate(
            [jnp.broadcast_to(ckn[h:h + 1, :], (SUBLANES, LANES)) for h in range(4)], axis=0)
        online(3, jnp.where(causal, lg[3] + bias, -jnp.inf), pvn(3))

    ppb = MOBA_BLOCK // page
    nb = SATTN_BLOCKS_PER_STEP

    def kt(t, m):
        return jnp.concatenate([r[m * W:(m + 1) * W, :] for r in k_refs[t * ppb:(t + 1) * ppb]],
                               axis=1).astype(BF16)

    def pv(t, m):
        vt = jnp.concatenate([r[m * W:(m + 1) * W, :] for r in v_refs[t * ppb:(t + 1) * ppb]],
                             axis=1).astype(BF16)
        return lambda p: lax.dot_general(p, vt, NT_DIMS, preferred_element_type=F32)

    def pv_step(m):
        def f(p):
            out = None
            for t in range(nb):
                d = pv(t, m)(p[:, t * MOBA_BLOCK:(t + 1) * MOBA_BLOCK])
                out = d if out is None else out + d
            return out
        return f

    lg = [[jnp.dot(qs_ref[m], kt(t, m), preferred_element_type=F32) for m in range(N_MIXERS)] for t in range(nb)]
    step_cols = lambda m: jnp.concatenate([lg[t][m] for t in range(nb)], axis=1)

    online(0, jnp.where(rows4(selp_ref[...]) > 0.5, step_cols(0), -jnp.inf), pv_step(0))
    ckp = ckp_ref[...]
    ck_rows = [jnp.broadcast_to(jnp.concatenate([ckp[tp:tp + 1, h * page:(h + 1) * page]
                                                 for tp in range(nb * ppb)], axis=1), (SUBLANES, nb * MOBA_BLOCK))
               for h in range(4)]
    online(3, step_cols(3) + (cqb_ref[:, 0:1] - jnp.concatenate(ck_rows, axis=0)), pv_step(3))

    for t in range(nb):
        block_partial(nb * (nsteps - 1 - j) + t, lg[t][1], pv(t, 1),
                      jnp.sum(lg[t][1], axis=1, keepdims=True) * (1.0 / MOBA_BLOCK))

    ones = jnp.ones((MOBA_BLOCK, LANES), BF16)
    carry = carry_ref[...]
    sps = [jnp.maximum(lg[t][2], 0.0) + jnp.log(1.0 + jnp.exp(-jnp.abs(lg[t][2]))) for t in range(nb)]
    acc_c = acc_ref[2]
    for t in reversed(range(nb)):
        log_keep = -sps[t]
        hi = log_keep.astype(BF16)
        lo = (log_keep - hi.astype(F32)).astype(BF16)
        later = (jnp.dot(hi, ust_ref[...], preferred_element_type=F32)
                 + jnp.dot(lo, ust_ref[...], preferred_element_type=F32)) + carry[:, 0:1]
        block_sum = jnp.dot(hi, ones, preferred_element_type=F32) + jnp.dot(lo, ones, preferred_element_type=F32)
        acc_c = acc_c + pv(t, 2)(jnp.exp(lg[t][2] - sps[t] + later).astype(BF16))
        carry = carry + block_sum
    acc_ref[2] = acc_c
    carry_ref[...] = carry

    @pl.when(j == nsteps - 1)
    def _():
        m_all = bst_ref[0]
        l_all = bst_ref[1]
        sc_all = bst_ref[2]
        beaten = jnp.zeros((R, LANES), F32)
        for m_ in range(nblk):
            bm = sc_all[:, m_:m_ + 1]
            wins = jnp.logical_or(bm > sc_all, jnp.logical_and(bm == sc_all, m_ < li))
            beaten = beaten + jnp.where(wins, 1.0, 0.0)
        chosen = jnp.logical_or(jnp.logical_and(li < nblk, beaten < float(MOBA_TOPK)), li == nblk)
        m_tot = jnp.max(jnp.where(chosen, m_all, NEG_BIG), axis=1, keepdims=True)
        wgt = jnp.where(chosen, jnp.exp(m_all - m_tot), 0.0)
        l_tot = jnp.sum(wgt * l_all, axis=1, keepdims=True)
        acc_b = jnp.zeros((R, W), F32)
        for n_ in range(nblk + 1):
            acc_b = acc_b + wgt[:, n_:n_ + 1] * bacc_ref[n_]
        res = [acc_ref[0] * (1.0 / ml_ref[0, :, 1:2]), acc_b * (1.0 / l_tot), acc_ref[2],
               acc_ref[3] * (1.0 / ml_ref[3, :, 1:2])]
        for m in range(N_MIXERS):
            r = jnp.where(own, res[m], 0.0)
            o_ref[:, m * W:(m + 1) * W] = (r[0:SUBLANES] + r[SUBLANES:2 * SUBLANES]
                                           + r[2 * SUBLANES:3 * SUBLANES] + r[3 * SUBLANES:])


def _sattn_call(page_table, q, kn, vn, sel, ckp, cqb, ckn, ust, k_pool_t, v_pool_t, *, lp, pool_base):
    n_dec, n_pages = page_table.shape
    page = k_pool_t.shape[2]
    nblk = n_pages * page // MOBA_BLOCK
    ppb = MOBA_BLOCK // page
    pps = ppb * SATTN_BLOCKS_PER_STEP
    nsteps = n_pages // pps
    assert nsteps * pps == n_pages
    past = n_pages * page
    tq = SUBLANES
    R = 4 * tq
    wstep = SATTN_BLOCKS_PER_STEP * MOBA_BLOCK
    blk = lambda w: pl.BlockSpec((tq, w), lambda b, j, pt: (b, 0))
    pg = lambda t: pl.BlockSpec((None, MIX_WIDTH, page),
                                lambda b, j, pt: (pt[b, pps * (nsteps - 1 - j) + t] + pool_base, 0, 0))
    ckp4 = ckp.reshape(n_dec, nsteps, pps, 4 * page)
    grid_spec = pltpu.PrefetchScalarGridSpec(
        num_scalar_prefetch=1,
        grid=(n_dec, nsteps),
        in_specs=[blk(MIX_WIDTH), blk(MIX_WIDTH), blk(MIX_WIDTH),
                  pl.BlockSpec((None, tq, wstep), lambda b, j, pt: (b, 0, nsteps - 1 - j)),
                  pl.BlockSpec((None, tq, LANES), lambda b, j, pt: (b, 0, past // LANES)),
                  pl.BlockSpec((None, None, pps, 4 * page), lambda b, j, pt: (b, nsteps - 1 - j, 0, 0)),
                  pl.BlockSpec((None, R, LANES), lambda b, j, pt: (b, 0, 0)),
                  pl.BlockSpec((None, SUBLANES, LANES), lambda b, j, pt: (b, 0, 0)),
                  pl.BlockSpec(ust.shape, lambda b, j, pt: (0, 0))]
                 + [pg(t) for t in range(pps)] * 2,
        out_specs=blk(MIX_WIDTH),
        scratch_shapes=[pltpu.VMEM((N_MIXERS, R, BRANCH_WIDTH), BF16),
                        pltpu.VMEM((N_MIXERS, R, LANES), F32),
                        pltpu.VMEM((N_MIXERS, R, BRANCH_WIDTH), F32),
                        pltpu.VMEM((R, LANES), F32),
                        pltpu.VMEM((3, R, LANES), F32),
                        pltpu.VMEM((nblk + 1, R, BRANCH_WIDTH), F32)],
    )
    return pl.pallas_call(
        functools.partial(_sattn_kernel, nblk=nblk, page=page),
        grid_spec=grid_spec,
        out_shape=jax.ShapeDtypeStruct((n_dec * tq, MIX_WIDTH), F32),
        compiler_params=pltpu.CompilerParams(dimension_semantics=("arbitrary", "arbitrary"),
                                             vmem_limit_bytes=VMEM_LIMIT),
        name="sample_attn",
    )(page_table, q, kn, vn, sel, sel, ckp4, cqb, ckn, ust, *([k_pool_t] * pps), *([v_pool_t] * pps))


def _rope_tables(pos):
    half = HEAD_DIM // 2
    inv = ROPE_THETA ** (-jnp.arange(half, dtype=F32) / half)
    ang = pos.astype(F32)[:, None] * inv[None, :]
    cos = jnp.cos(ang)
    sin = jnp.sin(ang)
    return jnp.tile(cos, (1, LANES // half)), jnp.tile(jnp.concatenate([-sin, sin], axis=1), (1, LANES // HEAD_DIM))


def _tri_consts():
    r = np.arange(MOBA_BLOCK)
    incl = (r[:, None] <= r[None, :]).astype(np.float32)
    strict_later = (r[:, None] > r[None, :]).astype(np.float32)
    return jnp.asarray(incl, BF16), jnp.asarray(strict_later, BF16)


def _row_cumsum_matrix(tm, seq_len):
    r = np.arange(tm)
    m = (r[None, :] <= r[:, None])
    if seq_len < tm:
        m = m & ((r[None, :] // seq_len) == (r[:, None] // seq_len))
    return jnp.asarray(m.astype(np.float32), BF16)


def kernel(x_prompt, x_sample, cache_k, cache_v, cache_kidx, cache_logf, state_conv, page_table,
           w_in, g_kidx, b_forget, w_branch, w_out, w_ffn_in, conv_w, conv_b, w_ffn_out,
           g_pre_mix, g_post_mix, g_pre_ffn, g_post_ffn):
    B, S, _ = x_prompt.shape
    DB, TS, _ = x_sample.shape
    depth = w_in.shape[0]
    n_pool, page = cache_k.shape[1], cache_k.shape[2]
    n_pages = page_table.shape[1]
    past = n_pages * page
    assert TS == SUBLANES and S % MOBA_BLOCK == 0 and page == LANES
    lp = -(-(past + TS) // MOBA_BLOCK) * MOBA_BLOCK

    cos_p, sin_p = _rope_tables(jnp.arange(S, dtype=jnp.int32))
    cos_s, sin_s = _rope_tables(jnp.tile(past + jnp.arange(TS, dtype=jnp.int32), DB))
    tri, ust = _tri_consts()
    ltri_p = _row_cumsum_matrix(PROJ_TM, S)
    ltri_s = _row_cumsum_matrix(PROJ_TM, TS)
    pr = np.arange(4 * page)
    head_prefix = jnp.asarray(((pr[:, None] // page == pr[None, :] // page)
                               & (pr[:, None] <= pr[None, :])).astype(np.float32), BF16)
    pgr = np.arange(n_pages)
    page_strict = jnp.asarray((pgr[None, :] < pgr[:, None]).astype(np.float32), BF16)

    k_pool = cache_k.transpose(0, 1, 3, 4, 2).reshape(depth * n_pool, MIX_WIDTH, page)
    v_pool = cache_v.transpose(0, 1, 3, 4, 2).reshape(depth * n_pool, MIX_WIDTH, page)
    kidx_pool = cache_kidx.transpose(0, 1, 3, 2).reshape(depth * n_pool, IDX_DIM, page)
    logf_pool = cache_logf.astype(F32).transpose(0, 1, 3, 2).reshape(depth * n_pool, 4 * page)

    yp = x_prompt.reshape(B * S, D_MODEL)
    ys = x_sample.reshape(DB * TS, D_MODEL)
    row2 = lambda v: v.reshape(1, -1)
    outs_p = [[] for _ in range(5)]
    outs_s = [[] for _ in range(5)]
    for l in range(depth):
        w = w_in[l]
        o0 = 3 * MIX_WIDTH
        o1 = o0 + IDX_HEADS * IDX_DIM
        o2 = o1 + IDX_DIM
        o3 = o2 + IDX_HEADS
        o4 = o3 + 4
        wqkv = w[:, :o0].astype(BF16)
        wqi = w[:, o0:o1].astype(BF16)
        wsm = jnp.concatenate([w[:, o1:o2], w[:, o1:o2], w[:, o2:o3], w[:, o3:o4], w[:, o3:o4],
                               jnp.zeros((D_MODEL, LANES - 12), F32)], axis=1).astype(BF16)
        wg = w[:, o4:].astype(BF16)
        wbr = w_branch[l].astype(BF16)
        wout = w_out[l].astype(BF16)
        wa = w_ffn_in[l][:, :D_FF].astype(BF16)
        wb = w_ffn_in[l][:, D_FF:].astype(BF16)
        wo = w_ffn_out[l].astype(BF16)
        cw = jnp.concatenate([conv_w[l], jnp.zeros((SUBLANES - CONV_WIDTH, D_FF), F32)], axis=0)
        cb = row2(conv_b[l])
        gk = row2(jnp.concatenate([g_kidx[l], g_kidx[l]]))
        bfp = row2(jnp.concatenate([jnp.zeros((4,), F32), b_forget[l], b_forget[l], jnp.zeros((LANES - 12,), F32)]))
        gpre, gpost = row2(g_pre_mix[l]), row2(g_post_mix[l])
        gpre_f, gpost_f = row2(g_pre_ffn[l]), row2(g_post_ffn[l])

        q, kf, vf, kb, vb, qi, kif, kib, misc, kbar = _proj_call(
            yp, gpre, wqkv, wqi, wsm, cos_p, sin_p, gk, bfp, ltri_p, seq_len=S, q_dtype=BF16)
        ckt = jnp.pad(misc[:, 8:12].reshape(B, S, 4).transpose(0, 2, 1), ((0, 0), (0, SUBLANES - 4), (0, 0)))
        ck4 = ckt.reshape(B, SUBLANES, S // ATTN_CK, ATTN_CK).transpose(0, 2, 1, 3)
        kbar_p = jnp.pad(kbar.reshape(B, S // MOBA_BLOCK, MIX_WIDTH).astype(BF16),
                         ((0, 0), (0, LANES - S // MOBA_BLOCK), (0, 0)))
        o = _pattn_call(q, qi, misc, kb, vb, kib, ck4, kbar_p, tri, ust, B=B, S=S)
        yp = _merge_call(yp, o, gpre, wg, wbr, wout, gpost)
        yp, alast = _ffn_call(yp, gpre_f, wa, wb, wo, cw, cb, gpost_f, seq_len=S)
        outs_p[0].append(kf.reshape(B, S, N_HEADS, HEAD_DIM))
        outs_p[1].append(vf.reshape(B, S, N_HEADS, HEAD_DIM))
        outs_p[2].append(kif[:, :IDX_DIM].reshape(B, S, IDX_DIM))
        outs_p[3].append(misc[:, 4:8].reshape(B, S, 4))
        alast = alast.reshape(B, -1, SUBLANES, D_FF)[:, -1]
        outs_p[4].append(alast[:, SUBLANES - (CONV_WIDTH - 1):, :])

        q, kf, vf, kb, vb, qi, kif, kib, misc, kbar = _proj_call(
            ys, gpre, wqkv, wqi, wsm, cos_s, sin_s, gk, bfp, ltri_s, seq_len=TS, q_dtype=F32)
        sel, ckp, cqb, ckn = _sidx_call(page_table, qi, misc, kif, tri, head_prefix, page_strict,
                                        kidx_pool, logf_pool, lp=lp, pool_base=l * n_pool)
        o = _sattn_call(page_table, q, kf, vf, sel, ckp, cqb, ckn, ust, k_pool, v_pool,
                        lp=lp, pool_base=l * n_pool)
        ys = _merge_call(ys, o, gpre, wg, wbr, wout, gpost)
        st = state_conv[l].astype(F32)
        zeros = jnp.zeros((DB, TS - 2, D_FF), F32)
        e1 = jnp.concatenate([st[:, 1:2], jnp.zeros((DB, TS - 1, D_FF), F32)], axis=1).reshape(DB * TS, D_FF)
        e2 = jnp.concatenate([st, zeros], axis=1).reshape(DB * TS, D_FF)
        ys, a_all = _ffn_call(ys, gpre_f, wa, wb, wo, cw, cb, gpost_f, seq_len=TS, state=(e1, e2))
        outs_s[0].append(kf.reshape(DB, TS, N_HEADS, HEAD_DIM))
        outs_s[1].append(vf.reshape(DB, TS, N_HEADS, HEAD_DIM))
        outs_s[2].append(kif[:, :IDX_DIM].reshape(DB, TS, IDX_DIM))
        outs_s[3].append(misc[:, 4:8].reshape(DB, TS, 4))
        outs_s[4].append(a_all.reshape(DB, TS, D_FF)[:, TS - (CONV_WIDTH - 1):, :])

    return (yp.reshape(B, S, D_MODEL), ys.reshape(DB, TS, D_MODEL),
            *[jnp.stack(v) for v in outs_p], *[jnp.stack(v) for v in outs_s])
```

```python
import functools

import numpy as np
import jax
import jax.numpy as jnp
from jax import lax
from jax.experimental import pallas as pl
from jax.experimental.pallas import tpu as pltpu

D_MODEL = 1024
HEAD_DIM = 64
N_HEADS = 16
MIX_WIDTH = N_HEADS * HEAD_DIM
BRANCH_WIDTH = 256
N_MIXERS = 4
IDX_HEADS = 4
IDX_DIM = 64
DSA_TOPK = 256
MOBA_BLOCK = 256
MOBA_TOPK = 3
ROPE_THETA = 10000.0
D_FF = 4 * D_MODEL
CONV_WIDTH = 3
NORM_EPS = 1e-6

LANES = 128
SUBLANES = 8
VMEM_LIMIT = 56 * 1024 * 1024

F32 = jnp.float32
BF16 = jnp.bfloat16
NT_DIMS = (((1,), (1,)), ((), ()))
NEG_BIG = -1e30
INT_MIN = -2147483648

PROJ_TM = 256
ATTN_TQ = 256
MERGE_TM = 512
FFN_TM = 512
FFN_TF = 1024


def _rms(x, g):
    return x * lax.rsqrt(jnp.mean(x * x, axis=-1, keepdims=True) + NORM_EPS) * g


def _softplus_neg_abs(x):
    return jnp.log1p(jnp.exp(-jnp.abs(x)))


def _split_dot(a, b_bf16):
    hi = a.astype(BF16)
    lo = (a - hi.astype(F32)).astype(BF16)
    return (jnp.dot(hi, b_bf16, preferred_element_type=F32)
            + jnp.dot(lo, b_bf16, preferred_element_type=F32))


def _split_dot_left(a_bf16, b):
    hi = b.astype(BF16)
    lo = (b - hi.astype(F32)).astype(BF16)
    return (jnp.dot(a_bf16, hi, preferred_element_type=F32)
            + jnp.dot(a_bf16, lo, preferred_element_type=F32))


def _sortable_key(score):
    bits = pltpu.bitcast(score + 0.0, jnp.int32)
    return bits ^ ((bits >> 31) & 0x7FFFFFFF)


def _kth_largest_key(key_ref, kth, rows):
    def body(it, lo):
        cand = lo + lax.shift_left(jnp.int32(1), 31 - it)
        cnt = jnp.sum(jnp.where(key_ref[...] >= cand, 1.0, 0.0), axis=1, keepdims=True)
        return jnp.where(cnt >= kth, cand, lo)
    return lax.fori_loop(0, 32, body, jnp.full((rows, 1), INT_MIN, jnp.int32))


def _topk_select(key_ref, tri_ref, kth, rows, width):
    thr = _kth_largest_key(key_ref, float(kth), rows)
    key = key_ref[...]
    gt = key > thr
    tie = key == thr
    need = float(kth) - jnp.sum(jnp.where(gt, 1.0, 0.0), axis=1, keepdims=True)
    off = jnp.zeros((rows, 1), F32)
    cw = tri_ref.shape[0]
    parts = []
    for c in range(width // cw):
        sl = slice(c * cw, (c + 1) * cw)
        tc = jnp.where(tie[:, sl], 1.0, 0.0).astype(BF16)
        within = jnp.dot(tc, tri_ref[...], preferred_element_type=F32)
        parts.append(jnp.logical_or(gt[:, sl], jnp.logical_and(tie[:, sl], within + off <= need)))
        off = off + within[:, cw - 1:cw]
    return jnp.concatenate(parts, axis=1)


def _proj_kernel(x_ref, g_ref, wqkv_ref, wqi_ref, wsm_ref, cos_ref, sin_ref, gk_ref, bf_ref, ltri_ref,
                 q_ref, kf_ref, vf_ref, kb_ref, vb_ref, qi_ref, kif_ref, kib_ref, misc_ref, kbar_ref,
                 carry_ref, *, tm, tiles_per_seq):
    i = pl.program_id(0)
    if tiles_per_seq > 1:
        @pl.when(i % tiles_per_seq == 0)
        def _():
            carry_ref[...] = jnp.zeros_like(carry_ref)
    hb = _rms(x_ref[...], g_ref[...]).astype(BF16)
    cos = cos_ref[...]
    sin = sin_ref[...]
    lane = lax.broadcasted_iota(jnp.int32, (tm, LANES), 1)
    first_half = (lane & (HEAD_DIM - 1)) < HEAD_DIM // 2

    def rope(t):
        rot = jnp.where(first_half, pltpu.roll(t, LANES - HEAD_DIM // 2, 1), pltpu.roll(t, HEAD_DIM // 2, 1))
        return t * cos + rot * sin

    n_rot = 4
    for c in range(MIX_WIDTH // LANES):
        sl = slice(c * LANES, (c + 1) * LANES)
        qc = jnp.dot(hb, wqkv_ref[:, sl], preferred_element_type=F32)
        kc = jnp.dot(hb, wqkv_ref[:, MIX_WIDTH + c * LANES:MIX_WIDTH + (c + 1) * LANES],
                     preferred_element_type=F32)
        vc = jnp.dot(hb, wqkv_ref[:, 2 * MIX_WIDTH + c * LANES:2 * MIX_WIDTH + (c + 1) * LANES],
                     preferred_element_type=F32)
        if c < n_rot:
            qc = rope(qc)
            kc = rope(kc)
        q_ref[:, sl] = (qc * HEAD_DIM ** -0.5).astype(q_ref.dtype)
        kf_ref[:, sl] = kc
        kb_ref[:, sl] = kc.astype(BF16)
        vf_ref[:, sl] = vc
        vb_ref[:, sl] = vc.astype(BF16)
        kbar_ref[:, sl] = jnp.mean(kc, axis=0, keepdims=True)
    for c in range(IDX_HEADS * IDX_DIM // LANES):
        sl = slice(c * LANES, (c + 1) * LANES)
        qic = rope(jnp.dot(hb, wqi_ref[:, sl], preferred_element_type=F32))
        qi_ref[:, sl] = (qic * IDX_DIM ** -0.5).astype(qi_ref.dtype)

    sm = jnp.dot(hb, wsm_ref[...], preferred_element_type=F32)
    ki = sm[:, :LANES]
    ki = rope(_rms(ki, gk_ref[...]))
    kif_ref[...] = ki
    kib_ref[...] = ki.astype(BF16)

    mi = sm[:, LANES:]
    lf = mi + bf_ref[...]
    lf = jnp.minimum(lf, 0.0) - _softplus_neg_abs(lf)
    is_cum = jnp.logical_and(lane >= 8, lane < 12)
    cum = _split_dot_left(ltri_ref[...], jnp.where(is_cum, lf, 0.0))
    if tiles_per_seq > 1:
        cum = cum + carry_ref[0:1, :]
        carry_ref[0:1, :] = cum[tm - 1:tm, :]
    misc = jnp.where(lane < 4, mi * IDX_HEADS ** -0.5,
                     jnp.where(lane < 8, lf, jnp.where(lane < 12, cum, 0.0)))
    misc_ref[...] = misc


def _proj_call(x, g, wqkv, wqi, wsm, cos_t, sin_t, gk, bfp, ltri, *, seq_len, q_dtype):
    T = x.shape[0]
    tm = PROJ_TM
    tiles_per_seq = max(seq_len // tm, 1)
    n_pos_tiles = cos_t.shape[0] // tm
    row = lambda w: pl.BlockSpec((tm, w), lambda i: (i, 0))
    const = lambda a: pl.BlockSpec(a.shape, lambda i: (0,) * a.ndim)
    pos = pl.BlockSpec((tm, LANES), lambda i: (i % n_pos_tiles, 0))
    out_shapes = (
        jax.ShapeDtypeStruct((T, MIX_WIDTH), q_dtype),
        jax.ShapeDtypeStruct((T, MIX_WIDTH), F32),
        jax.ShapeDtypeStruct((T, MIX_WIDTH), F32),
        jax.ShapeDtypeStruct((T, MIX_WIDTH), BF16),
        jax.ShapeDtypeStruct((T, MIX_WIDTH), BF16),
        jax.ShapeDtypeStruct((T, IDX_HEADS * IDX_DIM), q_dtype),
        jax.ShapeDtypeStruct((T, LANES), F32),
        jax.ShapeDtypeStruct((T, LANES), BF16),
        jax.ShapeDtypeStruct((T, LANES), F32),
        jax.ShapeDtypeStruct((T // tm, 1, MIX_WIDTH), F32),
    )
    out_specs = (row(MIX_WIDTH), row(MIX_WIDTH), row(MIX_WIDTH), row(MIX_WIDTH), row(MIX_WIDTH),
                 row(IDX_HEADS * IDX_DIM), row(LANES), row(LANES), row(LANES),
                 pl.BlockSpec((None, 1, MIX_WIDTH), lambda i: (i, 0, 0)))
    return pl.pallas_call(
        functools.partial(_proj_kernel, tm=tm, tiles_per_seq=tiles_per_seq),
        grid=(T // tm,),
        in_specs=[row(D_MODEL), const(g), const(wqkv), const(wqi), const(wsm), pos, pos,
                  const(gk), const(bfp), const(ltri)],
        out_specs=out_specs,
        out_shape=out_shapes,
        scratch_shapes=[pltpu.VMEM((SUBLANES, LANES), F32)],
        compiler_params=pltpu.CompilerParams(dimension_semantics=("arbitrary",),
                                             vmem_limit_bytes=VMEM_LIMIT),
        name="proj",
    )(x, g, wqkv, wqi, wsm, cos_t, sin_t, gk, bfp, ltri)


ATTN_CK = MOBA_BLOCK
N_PAIRS = N_HEADS // 2
STICK_PAIRS = (4, 5)
PAIR_GROUPS = ((0, 1, 2, 3), (4, 5, 6, 7))
SOFTMAX_SLOTS = 4


def _pattn_kernel(q_ref, qi_ref, misc_ref, wit_ref, k_ref, v_ref, ki_ref, ck_ref, kbar_ref, ltri_ref, ust_ref,
                  o_ref,
                  qst_ref, qist_ref, keyt_ref, sb_ref, lg_ref, mx_ref, ls_ref, acc_ref, cbias_ref, mb_ref,
                  cqb_ref, carry_ref, *, tq, S):
    ck = ATTN_CK
    nck = S // ck
    R = 2 * tq
    i = pl.program_id(1)
    q0 = i * tq
    nc = q0 // ck + 1
    cur = q0 // MOBA_BLOCK
    qpos = q0 + lax.broadcasted_iota(jnp.int32, (tq, 1), 0)
    colv = lax.broadcasted_iota(jnp.int32, (tq, ck), 1)
    lane = lax.broadcasted_iota(jnp.int32, (tq, LANES), 1)
    lo_half = lane < HEAD_DIM
    lane_r = lax.broadcasted_iota(jnp.int32, (R, LANES), 1)
    misc = misc_ref[...]

    def both_rows(x):
        return jnp.concatenate([x, x], axis=0)

    def both_halves(x):
        return jnp.concatenate([x] * (ck // LANES), axis=1)

    for p in range(N_PAIRS):
        blk = q_ref[:, p * LANES:(p + 1) * LANES]
        zero = jnp.zeros_like(blk)
        qst_ref[p, :tq, :] = jnp.where(lo_half, blk, zero)
        qst_ref[p, tq:, :] = jnp.where(lo_half, zero, blk)
    for h in range(IDX_HEADS):
        blk = qi_ref[:, (h // 2) * LANES:(h // 2 + 1) * LANES]
        zero = jnp.zeros_like(blk)
        qist_ref[h * tq:(h + 1) * tq, :] = jnp.where(lo_half, blk, zero) if h % 2 == 0 else jnp.where(lo_half, zero, blk)
    acc_ref[...] = jnp.zeros_like(acc_ref)
    carry_ref[...] = jnp.zeros_like(carry_ref)

    bi8 = lax.broadcasted_iota(jnp.int32, (SUBLANES, R), 0)
    for pb in range(2):
        p = 2 + pb
        sc8 = lax.dot_general(kbar_ref[:SUBLANES, p * LANES:(p + 1) * LANES], qst_ref[p], NT_DIMS,
                              preferred_element_type=F32)
        beaten = jnp.zeros((SUBLANES, R), F32)
        for m_ in range(nck):
            bm = jnp.sum(jnp.where(bi8 == m_, sc8, 0.0), axis=0, keepdims=True)
            wins = jnp.logical_or(bm > sc8, jnp.logical_and(bm == sc8, m_ < bi8))
            beaten = beaten + jnp.where(jnp.logical_and(wins, m_ < cur), 1.0, 0.0)
        chosen = jnp.logical_or(jnp.logical_and(bi8 < cur, beaten < float(MOBA_TOPK)), bi8 == cur)
        cb_t = jnp.concatenate([jnp.where(chosen, 0.0, NEG_BIG),
                                jnp.full((LANES - SUBLANES, R), NEG_BIG, F32)], axis=0)
        cbias_ref[pb] = cb_t.T
    for pd in range(2):
        cqb_ref[pd, :tq, :] = jnp.broadcast_to(misc[:, 8 + 2 * pd:9 + 2 * pd], (tq, LANES))
        cqb_ref[pd, tq:, :] = jnp.broadcast_to(misc[:, 9 + 2 * pd:10 + 2 * pd], (tq, LANES))

    qrow = q0 + lax.broadcasted_iota(jnp.int32, (1, tq), 1)
    krow = lax.broadcasted_iota(jnp.int32, (ck, tq), 0)
    wit = wit_ref[...]

    def score_body(c, carry):
        r0 = pl.multiple_of(c * ck, ck)
        rel = jnp.maximum(lax.dot_general(ki_ref[pl.ds(r0, ck), :], qist_ref[...], NT_DIMS,
                                          preferred_element_type=F32), 0.0)
        score = wit[0:1, :] * rel[:, 0:tq]
        for h in range(1, IDX_HEADS):
            score = score + wit[h:h + 1, :] * rel[:, h * tq:(h + 1) * tq]
        keyt_ref[c] = jnp.where(c * ck + krow <= qrow, _sortable_key(score), INT_MIN)
        return carry

    lax.fori_loop(0, nc, score_body, 0)

    def search(n_chunks):
        def count(pred):
            acc = jnp.zeros((SUBLANES, tq), F32)
            for c in range(n_chunks):
                hit = jnp.where(pred(keyt_ref[c]), 1.0, 0.0)
                acc = acc + jnp.sum(hit.reshape(ck // SUBLANES, SUBLANES, tq), axis=0)
            return jnp.sum(acc, axis=0, keepdims=True)

        def run():
            def bit_body(it, lo):
                cand = lo + lax.shift_left(jnp.int32(1), 31 - it)
                return jnp.where(count(lambda k_: k_ >= cand) >= float(DSA_TOPK), cand, lo)

            thr_ = lax.fori_loop(0, 32, bit_body, jnp.full((1, tq), INT_MIN, jnp.int32))
            return thr_, float(DSA_TOPK) - count(lambda k_: k_ > thr_)
        return run

    thr, need = lax.switch(nc - 1, [search(n) for n in range(1, nck + 1)])

    def tie_body(c, off):
        key = keyt_ref[c]
        tie = key == thr
        pre = jnp.dot(ltri_ref[...], jnp.where(tie, 1.0, 0.0).astype(BF16), preferred_element_type=F32) + off
        sel = jnp.logical_or(key > thr, jnp.logical_and(tie, pre <= need))
        sel = jnp.logical_and(sel, c * ck + krow <= qrow)
        sb_ref[c] = jnp.where(sel, 0.0, NEG_BIG).T
        return pre[ck - 1:ck, :]

    lax.fori_loop(0, nc, tie_body, jnp.zeros((1, tq), F32))

    def run_group(pairs):
        soft = [p for p in pairs if p not in STICK_PAIRS]
        slot = {p: n for n, p in enumerate(soft)}
        for p in soft:
            mx_ref[slot[p]] = jnp.full((R, ck), -3e38, F32)
            ls_ref[slot[p]] = jnp.zeros((R, ck), F32)

        def main_body(idx, carry):
            c = nc - 1 - idx
            r0 = pl.multiple_of(c * ck, ck)
            kpos = c * ck + colv
            cz2 = both_rows(jnp.where(kpos <= qpos, 0.0, NEG_BIG))
            strict2 = both_rows(kpos < qpos)
            for p in pairs:
                lg = lax.dot_general(qst_ref[p], k_ref[pl.ds(r0, ck), p * LANES:(p + 1) * LANES], NT_DIMS,
                                     preferred_element_type=F32)
                if p in STICK_PAIRS:
                    z = lg
                    sp = jnp.maximum(z, 0.0) + jnp.log(1.0 + jnp.exp(-jnp.abs(z)))
                    log_keep = jnp.where(strict2, -sp, 0.0)
                    cr = carry_ref[p - 4]
                    later = jnp.dot(log_keep.astype(BF16), ust_ref[...], preferred_element_type=F32) + both_halves(cr)
                    carry_ref[p - 4] = cr + jnp.sum(log_keep, axis=1, keepdims=True)
                    w = jnp.where(strict2, jnp.exp(z - sp + later), 0.0)
                    acc_ref[p] += jnp.dot(w.astype(BF16), v_ref[pl.ds(r0, ck), p * LANES:(p + 1) * LANES],
                                          preferred_element_type=F32)
                    continue
                if p < 2:
                    lg = lg + both_rows(sb_ref[c])
                elif p < 4:
                    col = jnp.sum(jnp.where(lane_r == c, cbias_ref[p - 2], 0.0), axis=1, keepdims=True)
                    lg = lg + col + cz2
                else:
                    pd = p - 6
                    ckr = ck_ref[c]
                    ck_rows = jnp.concatenate([jnp.broadcast_to(ckr[2 * pd:2 * pd + 1, :], (tq, ck)),
                                               jnp.broadcast_to(ckr[2 * pd + 1:2 * pd + 2, :], (tq, ck))], axis=0)
                    lg = lg + (both_halves(cqb_ref[pd]) - ck_rows) + cz2
                lg_ref[slot[p], c] = lg
                mx_ref[slot[p]] = jnp.maximum(mx_ref[slot[p]], lg)
            return carry

        lax.fori_loop(0, nc, main_body, 0)

        for p in soft:
            mb_ref[slot[p]] = jnp.broadcast_to(jnp.max(mx_ref[slot[p]], axis=1, keepdims=True), (R, LANES))

        def pv_body(c, carry):
            r0 = pl.multiple_of(c * ck, ck)
            for p in soft:
                pr = jnp.exp(lg_ref[slot[p], c] - both_halves(mb_ref[slot[p]]))
                ls_ref[slot[p]] += pr
                acc_ref[p] += jnp.dot(pr.astype(BF16), v_ref[pl.ds(r0, ck), p * LANES:(p + 1) * LANES],
                                      preferred_element_type=F32)
            return carry

        lax.fori_loop(0, nc, pv_body, 0)

        for p in pairs:
            a = acc_ref[p]
            if p in slot:
                a = a * (1.0 / jnp.sum(ls_ref[slot[p]], axis=1, keepdims=True))
            o_ref[:, p * LANES:(p + 1) * LANES] = jnp.where(lo_half, a[:tq], a[tq:]).astype(o_ref.dtype)

    for pairs in PAIR_GROUPS:
        run_group(pairs)


def _pattn_call(q, qi, misc, wit, kb, vb, kib, ck4, kbar, ltri, ust, *, B, S):
    tq = ATTN_TQ
    ck = ATTN_CK
    nq = S // tq
    nck = S // ck
    T = B * S
    R = 2 * tq
    n_soft = SOFTMAX_SLOTS
    qrow = lambda w: pl.BlockSpec((tq, w), lambda b, i: (b * nq + i, 0))
    seq = lambda w: pl.BlockSpec((S, w), lambda b, i: (b, 0), pipeline_mode=pl.Buffered(1))
    const = lambda a: pl.BlockSpec(a.shape, lambda b, i: (0,) * a.ndim)
    return pl.pallas_call(
        functools.partial(_pattn_kernel, tq=tq, S=S),
        grid=(B, nq),
        in_specs=[qrow(MIX_WIDTH), qrow(IDX_HEADS * IDX_DIM), qrow(LANES),
                  pl.BlockSpec((None, SUBLANES, tq), lambda b, i: (b, 0, i)), seq(MIX_WIDTH), seq(MIX_WIDTH),
                  seq(LANES), pl.BlockSpec((None, nck, SUBLANES, ck), lambda b, i: (b, 0, 0, 0)),
                  pl.BlockSpec((None, LANES, MIX_WIDTH), lambda b, i: (b, 0, 0)), const(ltri), const(ust)],
        out_specs=qrow(MIX_WIDTH),
        out_shape=jax.ShapeDtypeStruct((T, MIX_WIDTH), BF16),
        scratch_shapes=[pltpu.VMEM((N_PAIRS, R, LANES), BF16),
                        pltpu.VMEM((IDX_HEADS * tq, LANES), BF16),
                        pltpu.VMEM((nck, ck, tq), jnp.int32),
                        pltpu.VMEM((nck, tq, ck), F32),
                        pltpu.VMEM((n_soft, nck, R, ck), F32),
                        pltpu.VMEM((n_soft, R, ck), F32),
                        pltpu.VMEM((n_soft, R, ck), F32),
                        pltpu.VMEM((N_PAIRS, R, LANES), F32),
                        pltpu.VMEM((2, R, LANES), F32),
                        pltpu.VMEM((n_soft, R, LANES), F32),
                        pltpu.VMEM((2, R, LANES), F32),
                        pltpu.VMEM((2, R, LANES), F32)],
        compiler_params=pltpu.CompilerParams(dimension_semantics=("arbitrary", "arbitrary"),
                                             vmem_limit_bytes=VMEM_LIMIT),
        name="prompt_attn",
    )(q, qi, misc, wit, kb, vb, kib, ck4, kbar, ltri, ust)


def _merge_kernel(x_ref, o_ref, gpre_ref, wg_ref, wbr_ref, wout_ref, gpost_ref, y_ref):
    x = x_ref[...]
    hb = _rms(x, gpre_ref[...]).astype(BF16)
    mix = None
    for m in range(N_MIXERS):
        gl = jnp.dot(hb, wg_ref[:, m * D_MODEL:(m + 1) * D_MODEL], preferred_element_type=F32)
        br = jnp.dot(o_ref[:, m * BRANCH_WIDTH:(m + 1) * BRANCH_WIDTH].astype(BF16), wbr_ref[m],
                     preferred_element_type=F32)
        term = jax.nn.sigmoid(gl) * br
        mix = term if mix is None else mix + term
    out = jnp.dot(mix.astype(BF16), wout_ref[...], preferred_element_type=F32)
    y_ref[...] = x + _rms(out, gpost_ref[...])


def _merge_call(x, o, gpre, wg, wbr, wout, gpost):
    T = x.shape[0]
    tm = min(MERGE_TM, T)
    assert T % tm == 0
    row = lambda w: pl.BlockSpec((tm, w), lambda i: (i, 0))
    const = lambda a: pl.BlockSpec(a.shape, lambda i: (0,) * a.ndim)
    return pl.pallas_call(
        _merge_kernel,
        grid=(T // tm,),
        in_specs=[row(D_MODEL), row(MIX_WIDTH), const(gpre), const(wg), const(wbr), const(wout), const(gpost)],
        out_specs=row(D_MODEL),
        out_shape=jax.ShapeDtypeStruct((T, D_MODEL), F32),
        compiler_params=pltpu.CompilerParams(dimension_semantics=("arbitrary",),
                                             vmem_limit_bytes=VMEM_LIMIT),
        name="merge",
    )(x, o, gpre, wg, wbr, wout, gpost)


def _ffn_kernel(*refs, tm, tiles_per_seq, has_state):
    if has_state:
        (x_ref, gpre_ref, wa_ref, wb_ref, wo_ref, cw_ref, cb_ref, gpost_ref, e1_ref, e2_ref,
         y_ref, a_ref, hb_ref, acc_ref) = refs
    else:
        (x_ref, gpre_ref, wa_ref, wb_ref, wo_ref, cw_ref, cb_ref, gpost_ref,
         y_ref, a_ref, hb_ref, acc_ref, prev_ref) = refs
    i = pl.program_id(0)
    f = pl.program_id(1)
    nf = pl.num_programs(1)

    @pl.when(f == 0)
    def _():
        hb_ref[...] = _rms(x_ref[...], gpre_ref[...]).astype(BF16)
        acc_ref[...] = jnp.zeros_like(acc_ref)

    if not has_state:
        @pl.when(i % tiles_per_seq == 0)
        def _():
            prev_ref[f] = jnp.zeros(prev_ref.shape[1:], F32)

    hb = hb_ref[...]
    a = jnp.dot(hb, wa_ref[...], preferred_element_type=F32)
    b = jnp.dot(hb, wb_ref[...], preferred_element_type=F32)
    rowi = lax.broadcasted_iota(jnp.int32, a.shape, 0)
    r1 = pltpu.roll(a, 1, 0)
    r2 = pltpu.roll(a, 2, 0)
    if has_state:
        t = rowi & (SUBLANES - 1)
        p1 = jnp.where(t >= 1, r1, e1_ref[...])
        p2 = jnp.where(t >= 2, r2, e2_ref[...])
        a_ref[...] = a
    else:
        prev = prev_ref[f]
        p1 = jnp.where(rowi >= 1, r1, prev[SUBLANES - 1:SUBLANES, :])
        p2 = jnp.where(rowi >= 2, r2, jnp.where(rowi == 0, prev[SUBLANES - 2:SUBLANES - 1, :],
                                                 prev[SUBLANES - 1:SUBLANES, :]))
        last = a[tm - SUBLANES:, :]
        prev_ref[f] = last
        a_ref[...] = last
    c = cb_ref[...] + p2 * cw_ref[0:1, :] + p1 * cw_ref[1:2, :] + a * cw_ref[2:3, :]
    gated = (jax.nn.gelu(c, approximate=True) * b).astype(BF16)
    acc_ref[...] += jnp.dot(gated, wo_ref[...], preferred_element_type=F32)

    @pl.when(f == nf - 1)
    def _():
        y_ref[...] = x_ref[...] + _rms(acc_ref[...], gpost_ref[...])


def _ffn_call(x, gpre, wa, wb, wo, cw, cb, gpost, *, seq_len, state=None):
    T = x.shape[0]
    tm = min(FFN_TM, T)
    tf = FFN_TF
    nf = D_FF // tf
    has_state = state is not None
    tiles_per_seq = max(seq_len // tm, 1)
    in_specs = [pl.BlockSpec((tm, D_MODEL), lambda i, f: (i, 0)),
                pl.BlockSpec((1, D_MODEL), lambda i, f: (0, 0)),
                pl.BlockSpec((D_MODEL, tf), lambda i, f: (0, f)),
                pl.BlockSpec((D_MODEL, tf), lambda i, f: (0, f)),
                pl.BlockSpec((tf, D_MODEL), lambda i, f: (f, 0)),
                pl.BlockSpec((SUBLANES, tf), lambda i, f: (0, f)),
                pl.BlockSpec((1, tf), lambda i, f: (0, f)),
                pl.BlockSpec((1, D_MODEL), lambda i, f: (0, 0))]
    args = [x, gpre, wa, wb, wo, cw, cb, gpost]
    scratch = [pltpu.VMEM((tm, D_MODEL), BF16), pltpu.VMEM((tm, D_MODEL), F32)]
    if has_state:
        in_specs += [pl.BlockSpec((tm, tf), lambda i, f: (i, f))] * 2
        args += list(state)
        a_shape = jax.ShapeDtypeStruct((T, D_FF), F32)
        a_spec = pl.BlockSpec((tm, tf), lambda i, f: (i, f))
    else:
        a_shape = jax.ShapeDtypeStruct((T // tm, SUBLANES, D_FF), F32)
        a_spec = pl.BlockSpec((None, SUBLANES, tf), lambda i, f: (i, 0, f))
        scratch.append(pltpu.VMEM((nf, SUBLANES, tf), F32))
    return pl.pallas_call(
        functools.partial(_ffn_kernel, tm=tm, tiles_per_seq=tiles_per_seq, has_state=has_state),
        grid=(T // tm, nf),
        in_specs=in_specs,
        out_specs=(pl.BlockSpec((tm, D_MODEL), lambda i, f: (i, 0)), a_spec),
        out_shape=(jax.ShapeDtypeStruct((T, D_MODEL), F32), a_shape),
        scratch_shapes=scratch,
        compiler_params=pltpu.CompilerParams(dimension_semantics=("arbitrary", "arbitrary"),
                                             vmem_limit_bytes=VMEM_LIMIT),
        name="ffn",
    )(*args)


SIDX_SEQS = 4


def _sidx_kernel(pt_ref, qi_ref, misc_ref, kin_ref, tri_ref, hp_ref, ls_ref, kidx_hbm, logf_hbm,
                 sel_ref, ckp_ref, cqb_ref, ckn_ref, kbuf, lbuf, key_ref, sem, *, n_pages, page, lp, pool_base):
    G = SIDX_SEQS
    b = pl.program_id(0)
    n_steps = pl.num_programs(0)
    past = n_pages * page
    tq = SUBLANES
    slot = b % 2

    def copies(step, slot_, g, p):
        pg = pt_ref[step * G + g, p] + pool_base
        return (pltpu.make_async_copy(kidx_hbm.at[pg], kbuf.at[slot_, g, :, pl.ds(p * page, page)], sem.at[slot_, 0]),
                pltpu.make_async_copy(logf_hbm.at[pl.ds(pg, 1)], lbuf.at[slot_, g, pl.ds(p, 1)], sem.at[slot_, 1]))

    def start_all(step, slot_):
        for g in range(G):
            for p in range(n_pages):
                for cp in copies(step, slot_, g, p):
                    cp.start()

    @pl.when(b == 0)
    def _():
        start_all(0, 0)

    @pl.when(b + 1 < n_steps)
    def _():
        start_all(b + 1, 1 - slot)

    for g in range(G):
        for p in range(n_pages):
            for cp in copies(b, slot, g, p):
                cp.wait()

    qrow = lax.broadcasted_iota(jnp.int32, (tq, LANES), 0)
    ncol = lax.broadcasted_iota(jnp.int32, (tq, LANES), 1)
    for g in range(G):
        rows = slice(g * tq, (g + 1) * tq)
        qi = qi_ref[rows, :]
        qall = jnp.concatenate([qi[:, h * IDX_DIM:(h + 1) * IDX_DIM] for h in range(IDX_HEADS)], axis=0).astype(BF16)
        rel = jnp.maximum(jnp.dot(qall, kbuf[slot, g].astype(BF16), preferred_element_type=F32), 0.0)
        knew = jnp.concatenate([kin_ref[rows, :IDX_DIM], jnp.zeros((LANES - tq, IDX_DIM), F32)], axis=0).astype(BF16)
        reln = jnp.maximum(lax.dot_general(qall, knew, NT_DIMS, preferred_element_type=F32), 0.0)
        misc = misc_ref[rows, :]
        sp = jnp.zeros((tq, past), F32)
        sn = jnp.zeros((tq, LANES), F32)
        for h in range(IDX_HEADS):
            sp = sp + misc[:, h:h + 1] * rel[h * tq:(h + 1) * tq, :]
            sn = sn + misc[:, h:h + 1] * reln[h * tq:(h + 1) * tq, :]
        key_ref[rows, :past] = _sortable_key(sp)
        key_ref[rows, past:past + LANES] = jnp.where(ncol <= qrow, _sortable_key(sn), INT_MIN)
        if lp > past + LANES:
            key_ref[rows, past + LANES:] = jnp.full((tq, lp - past - LANES), INT_MIN, jnp.int32)

        lg = lbuf[slot, g]
        within = _split_dot(lg, hp_ref[...])
        totals = jnp.concatenate(
            [jnp.broadcast_to(within[:, (h + 1) * page - 1:(h + 1) * page], (n_pages, page)) for h in range(4)], axis=1)
        offs = _split_dot_left(ls_ref[...], totals)
        ckp = within + offs
        ckp_ref[g] = ckp
        tot = ckp[n_pages - 1:n_pages, :]
        cq_rows = []
        ckn_rows = []
        mt = jnp.concatenate([misc, jnp.zeros((LANES - tq, LANES), F32)], axis=0).T
        for h in range(4):
            tot_h = tot[:, (h + 1) * page - 1:(h + 1) * page]
            cq_rows.append(jnp.broadcast_to(misc[:, 8 + h:9 + h] + tot_h, (tq, LANES)))
            ckn_rows.append(mt[8 + h:9 + h, :] + tot_h)
        cqb_ref[g] = jnp.concatenate(cq_rows, axis=0)
        ckn_ref[g] = jnp.concatenate(ckn_rows + [jnp.zeros((SUBLANES - 4, LANES), F32)], axis=0)

    sel = _topk_select(key_ref, tri_ref, DSA_TOPK, G * tq, lp)
    col = lax.broadcasted_iota(jnp.int32, (G * tq, lp), 1)
    qr = lax.broadcasted_iota(jnp.int32, (G * tq, lp), 0) & (tq - 1)
    selv = jnp.where(jnp.logical_and(sel, col <= past + qr), 1.0, 0.0)
    for g in range(G):
        sel_ref[g] = selv[g * tq:(g + 1) * tq, :]


def _sidx_call(page_table, qi, misc, kif, tri, hp, ls, kidx_pool_t, logf_pool, *, lp, pool_base):
    n_dec, n_pages = page_table.shape
    page = kidx_pool_t.shape[2]
    tq = SUBLANES
    G = SIDX_SEQS
    assert n_dec % G == 0
    blk = lambda w: pl.BlockSpec((G * tq, w), lambda b, pt: (b, 0))
    const = lambda a: pl.BlockSpec(a.shape, lambda b, pt: (0,) * a.ndim)
    grid_spec = pltpu.PrefetchScalarGridSpec(
        num_scalar_prefetch=1,
        grid=(n_dec // G,),
        in_specs=[blk(IDX_HEADS * IDX_DIM), blk(LANES), blk(LANES), const(tri), const(hp), const(ls),
                  pl.BlockSpec(memory_space=pl.ANY), pl.BlockSpec(memory_space=pl.ANY)],
        out_specs=(pl.BlockSpec((G, tq, lp), lambda b, pt: (b, 0, 0)),
                   pl.BlockSpec((G, n_pages, 4 * page), lambda b, pt: (b, 0, 0)),
                   pl.BlockSpec((G, 4 * tq, LANES), lambda b, pt: (b, 0, 0)),
                   pl.BlockSpec((G, SUBLANES, LANES), lambda b, pt: (b, 0, 0))),
        scratch_shapes=[pltpu.VMEM((2, G, IDX_DIM, n_pages * page), F32),
                        pltpu.VMEM((2, G, n_pages, 4 * page), F32),
                        pltpu.VMEM((G * tq, lp), jnp.int32),
                        pltpu.SemaphoreType.DMA((2, 2))],
    )
    return pl.pallas_call(
        functools.partial(_sidx_kernel, n_pages=n_pages, page=page, lp=lp, pool_base=pool_base),
        grid_spec=grid_spec,
        out_shape=(jax.ShapeDtypeStruct((n_dec, tq, lp), F32),
                   jax.ShapeDtypeStruct((n_dec, n_pages, 4 * page), F32),
                   jax.ShapeDtypeStruct((n_dec, 4 * tq, LANES), F32),
                   jax.ShapeDtypeStruct((n_dec, SUBLANES, LANES), F32)),
        compiler_params=pltpu.CompilerParams(dimension_semantics=("arbitrary",),
                                             vmem_limit_bytes=VMEM_LIMIT),
        name="sample_index",
    )(page_table, qi, misc, kif, tri, hp, ls, kidx_pool_t, logf_pool)


SATTN_BLOCKS_PER_STEP = 4


def _sattn_kernel(pt_ref, q_ref, kn_ref, vn_ref, selp_ref, seln_ref, ckp_ref, cqb_ref, ckn_ref, ust_ref,
                  *rest, nblk, page):
    npg = SATTN_BLOCKS_PER_STEP * (MOBA_BLOCK // page)
    k_refs = rest[:npg]
    v_refs = rest[npg:2 * npg]
    o_ref, qs_ref, ml_ref, acc_ref, carry_ref, bst_ref, bacc_ref = rest[2 * npg:]
    j = pl.program_id(1)
    nsteps = pl.num_programs(1)
    R = 4 * SUBLANES
    W = BRANCH_WIDTH
    rowi = lax.broadcasted_iota(jnp.int32, (R, W), 0)
    lanei = lax.broadcasted_iota(jnp.int32, (R, W), 1)
    own = (rowi // SUBLANES) == (lanei // HEAD_DIM)
    li = lax.broadcasted_iota(jnp.int32, (R, LANES), 1)

    def online(m_idx, lg, pv):
        m_old = ml_ref[m_idx, :, 0:1]
        l_old = ml_ref[m_idx, :, 1:2]
        m_new = jnp.maximum(m_old, jnp.max(lg, axis=1, keepdims=True))
        alpha = jnp.exp(m_old - m_new)
        p = jnp.exp(lg - m_new)
        ml_ref[m_idx, :, 0:1] = m_new
        ml_ref[m_idx, :, 1:2] = alpha * l_old + jnp.sum(p, axis=1, keepdims=True)
        acc_ref[m_idx] = alpha * acc_ref[m_idx] + pv(p.astype(BF16))

    def block_partial(n, lg, pv, score):
        m_n = jnp.max(lg, axis=1, keepdims=True)
        p = jnp.exp(lg - m_n)
        l_n = jnp.sum(p, axis=1, keepdims=True)
        here = li == n
        bst_ref[0] = jnp.where(here, m_n, bst_ref[0])
        bst_ref[1] = jnp.where(here, l_n, bst_ref[1])
        bst_ref[2] = jnp.where(here, score, bst_ref[2])
        bacc_ref[n] = pv(p.astype(BF16))

    def stick(z, strict, pv, ust):
        sp = jnp.maximum(z, 0.0) + jnp.log(1.0 + jnp.exp(-jnp.abs(z)))
        log_keep = -sp
        if strict is not None:
            log_keep = jnp.where(strict, log_keep, 0.0)
        carry = carry_ref[...]
        later = _split_dot(log_keep, ust) + carry[:, 0:1]
        carry_ref[...] = carry + jnp.sum(log_keep, axis=1, keepdims=True)
        w = jnp.exp(z - sp + later)
        if strict is not None:
            w = jnp.where(strict, w, 0.0)
        acc_ref[2] = acc_ref[2] + pv(w.astype(BF16))

    def rows4(x8):
        return jnp.concatenate([x8] * 4, axis=0)

    @pl.when(j == 0)
    def _():
        q = q_ref[...]
        for m in range(N_MIXERS):
            qm = rows4(q[:, m * W:(m + 1) * W])
            qs_ref[m] = jnp.where(own, qm, 0.0).astype(BF16)
        li3 = lax.broadcasted_iota(jnp.int32, (N_MIXERS, R, LANES), 2)
        ml_ref[...] = jnp.where(li3 == 0, NEG_BIG, 0.0)
        acc_ref[...] = jnp.zeros_like(acc_ref)
        carry_ref[...] = jnp.zeros_like(carry_ref)
        bst_ref[...] = jnp.zeros_like(bst_ref)
        pad = jnp.zeros((LANES - SUBLANES, MIX_WIDTH), F32)
        kn = jnp.concatenate([kn_ref[...], pad], axis=0).astype(BF16)
        vn = jnp.concatenate([vn_ref[...], pad], axis=0).astype(BF16)
        kcol = lax.broadcasted_iota(jnp.int32, (R, LANES), 1)
        qq = lax.broadcasted_iota(jnp.int32, (R, LANES), 0) & (SUBLANES - 1)
        causal = kcol <= qq
        strict = kcol < qq
        lg = [lax.dot_general(qs_ref[m], kn[:, m * W:(m + 1) * W], NT_DIMS, preferred_element_type=F32)
              for m in range(N_MIXERS)]
        pvn = lambda m: (lambda p: jnp.dot(p, vn[:, m * W:(m + 1) * W], preferred_element_type=F32))
        sel = rows4(seln_ref[...]) > 0.5
        online(0, jnp.where(jnp.logical_and(sel, causal), lg[0], -jnp.inf), pvn(0))
        block_partial(nblk, jnp.where(causal, lg[1], NEG_BIG), pvn(1), jnp.zeros((R, 1), F32))
        stick(lg[2], strict, pvn(2), ust_ref[:LANES, :LANES])
        ckn = ckn_ref[...]
        bias = cqb_ref[...] - jnp.concatenate(
            [jnp.broadcast_to(ckn[h:h + 1, :], (SUBLANES, LANES)) for h in range(4)], axis=0)
        online(3, jnp.where(causal, lg[3] + bias, -jnp.inf), pvn(3))

    ppb = MOBA_BLOCK // page
    nb = SATTN_BLOCKS_PER_STEP

    def kt(t, m):
        return jnp.concatenate([r[m * W:(m + 1) * W, :] for r in k_refs[t * ppb:(t + 1) * ppb]],
                               axis=1).astype(BF16)

    def pv(t, m):
        vt = jnp.concatenate([r[m * W:(m + 1) * W, :] for r in v_refs[t * ppb:(t + 1) * ppb]],
                             axis=1).astype(BF16)
        return lambda p: lax.dot_general(p, vt, NT_DIMS, preferred_element_type=F32)

    def pv_step(m):
        def f(p):
            out = None
            for t in range(nb):
                d = pv(t, m)(p[:, t * MOBA_BLOCK:(t + 1) * MOBA_BLOCK])
                out = d if out is None else out + d
            return out
        return f

    lg = [[jnp.dot(qs_ref[m], kt(t, m), preferred_element_type=F32) for m in range(N_MIXERS)] for t in range(nb)]
    step_cols = lambda m: jnp.concatenate([lg[t][m] for t in range(nb)], axis=1)

    online(0, jnp.where(rows4(selp_ref[...]) > 0.5, step_cols(0), -jnp.inf), pv_step(0))
    ckp = ckp_ref[...]
    ck_rows = [jnp.broadcast_to(jnp.concatenate([ckp[tp:tp + 1, h * page:(h + 1) * page]
                                                 for tp in range(nb * ppb)], axis=1), (SUBLANES, nb * MOBA_BLOCK))
               for h in range(4)]
    online(3, step_cols(3) + (cqb_ref[:, 0:1] - jnp.concatenate(ck_rows, axis=0)), pv_step(3))

    for t in range(nb):
        block_partial(nb * (nsteps - 1 - j) + t, lg[t][1], pv(t, 1),
                      jnp.sum(lg[t][1], axis=1, keepdims=True) * (1.0 / MOBA_BLOCK))

    ones = jnp.ones((MOBA_BLOCK, LANES), BF16)
    carry = carry_ref[...]
    sps = [jnp.maximum(lg[t][2], 0.0) + jnp.log(1.0 + jnp.exp(-jnp.abs(lg[t][2]))) for t in range(nb)]
    acc_c = acc_ref[2]
    for t in reversed(range(nb)):
        log_keep = -sps[t]
        hi = log_keep.astype(BF16)
        lo = (log_keep - hi.astype(F32)).astype(BF16)
        later = (jnp.dot(hi, ust_ref[...], preferred_element_type=F32)
                 + jnp.dot(lo, ust_ref[...], preferred_element_type=F32)) + carry[:, 0:1]
        block_sum = jnp.dot(hi, ones, preferred_element_type=F32) + jnp.dot(lo, ones, preferred_element_type=F32)
        acc_c = acc_c + pv(t, 2)(jnp.exp(lg[t][2] - sps[t] + later).astype(BF16))
        carry = carry + block_sum
    acc_ref[2] = acc_c
    carry_ref[...] = carry

    @pl.when(j == nsteps - 1)
    def _():
        m_all = bst_ref[0]
        l_all = bst_ref[1]
        sc_all = bst_ref[2]
        beaten = jnp.zeros((R, LANES), F32)
        for m_ in range(nblk):
            bm = sc_all[:, m_:m_ + 1]
            wins = jnp.logical_or(bm > sc_all, jnp.logical_and(bm == sc_all, m_ < li))
            beaten = beaten + jnp.where(wins, 1.0, 0.0)
        chosen = jnp.logical_or(jnp.logical_and(li < nblk, beaten < float(MOBA_TOPK)), li == nblk)
        m_tot = jnp.max(jnp.where(chosen, m_all, NEG_BIG), axis=1, keepdims=True)
        wgt = jnp.where(chosen, jnp.exp(m_all - m_tot), 0.0)
        l_tot = jnp.sum(wgt * l_all, axis=1, keepdims=True)
        acc_b = jnp.zeros((R, W), F32)
        for n_ in range(nblk + 1):
            acc_b = acc_b + wgt[:, n_:n_ + 1] * bacc_ref[n_]
        res = [acc_ref[0] * (1.0 / ml_ref[0, :, 1:2]), acc_b * (1.0 / l_tot), acc_ref[2],
               acc_ref[3] * (1.0 / ml_ref[3, :, 1:2])]
        for m in range(N_MIXERS):
            r = jnp.where(own, res[m], 0.0)
            o_ref[:, m * W:(m + 1) * W] = (r[0:SUBLANES] + r[SUBLANES:2 * SUBLANES]
                                           + r[2 * SUBLANES:3 * SUBLANES] + r[3 * SUBLANES:])


def _sattn_call(page_table, q, kn, vn, sel, ckp, cqb, ckn, ust, k_pool_t, v_pool_t, *, lp, pool_base):
    n_dec, n_pages = page_table.shape
    page = k_pool_t.shape[2]
    nblk = n_pages * page // MOBA_BLOCK
    ppb = MOBA_BLOCK // page
    pps = ppb * SATTN_BLOCKS_PER_STEP
    nsteps = n_pages // pps
    assert nsteps * pps == n_pages
    past = n_pages * page
    tq = SUBLANES
    R = 4 * tq
    wstep = SATTN_BLOCKS_PER_STEP * MOBA_BLOCK
    blk = lambda w: pl.BlockSpec((tq, w), lambda b, j, pt: (b, 0))
    pg = lambda t: pl.BlockSpec((None, MIX_WIDTH, page),
                                lambda b, j, pt: (pt[b, pps * (nsteps - 1 - j) + t] + pool_base, 0, 0))
    ckp4 = ckp.reshape(n_dec, nsteps, pps, 4 * page)
    grid_spec = pltpu.PrefetchScalarGridSpec(
        num_scalar_prefetch=1,
        grid=(n_dec, nsteps),
        in_specs=[blk(MIX_WIDTH), blk(MIX_WIDTH), blk(MIX_WIDTH),
                  pl.BlockSpec((None, tq, wstep), lambda b, j, pt: (b, 0, nsteps - 1 - j)),
                  pl.BlockSpec((None, tq, LANES), lambda b, j, pt: (b, 0, past // LANES)),
                  pl.BlockSpec((None, None, pps, 4 * page), lambda b, j, pt: (b, nsteps - 1 - j, 0, 0)),
                  pl.BlockSpec((None, R, LANES), lambda b, j, pt: (b, 0, 0)),
                  pl.BlockSpec((None, SUBLANES, LANES), lambda b, j, pt: (b, 0, 0)),
                  pl.BlockSpec(ust.shape, lambda b, j, pt: (0, 0))]
                 + [pg(t) for t in range(pps)] * 2,
        out_specs=blk(MIX_WIDTH),
        scratch_shapes=[pltpu.VMEM((N_MIXERS, R, BRANCH_WIDTH), BF16),
                        pltpu.VMEM((N_MIXERS, R, LANES), F32),
                        pltpu.VMEM((N_MIXERS, R, BRANCH_WIDTH), F32),
                        pltpu.VMEM((R, LANES), F32),
                        pltpu.VMEM((3, R, LANES), F32),
                        pltpu.VMEM((nblk + 1, R, BRANCH_WIDTH), F32)],
    )
    return pl.pallas_call(
        functools.partial(_sattn_kernel, nblk=nblk, page=page),
        grid_spec=grid_spec,
        out_shape=jax.ShapeDtypeStruct((n_dec * tq, MIX_WIDTH), F32),
        compiler_params=pltpu.CompilerParams(dimension_semantics=("arbitrary", "arbitrary"),
                                             vmem_limit_bytes=VMEM_LIMIT),
        name="sample_attn",
    )(page_table, q, kn, vn, sel, sel, ckp4, cqb, ckn, ust, *([k_pool_t] * pps), *([v_pool_t] * pps))


def _rope_tables(pos):
    half = HEAD_DIM // 2
    inv = ROPE_THETA ** (-jnp.arange(half, dtype=F32) / half)
    ang = pos.astype(F32)[:, None] * inv[None, :]
    cos = jnp.cos(ang)
    sin = jnp.sin(ang)
    return jnp.tile(cos, (1, LANES // half)), jnp.tile(jnp.concatenate([-sin, sin], axis=1), (1, LANES // HEAD_DIM))


def _tri_consts():
    r = np.arange(MOBA_BLOCK)
    incl = (r[:, None] <= r[None, :]).astype(np.float32)
    strict_later = (r[:, None] > r[None, :]).astype(np.float32)
    return jnp.asarray(incl, BF16), jnp.asarray(strict_later, BF16)


def _row_cumsum_matrix(tm, seq_len):
    r = np.arange(tm)
    m = (r[None, :] <= r[:, None])
    if seq_len < tm:
        m = m & ((r[None, :] // seq_len) == (r[:, None] // seq_len))
    return jnp.asarray(m.astype(np.float32), BF16)


def kernel(x_prompt, x_sample, cache_k, cache_v, cache_kidx, cache_logf, state_conv, page_table,
           w_in, g_kidx, b_forget, w_branch, w_out, w_ffn_in, conv_w, conv_b, w_ffn_out,
           g_pre_mix, g_post_mix, g_pre_ffn, g_post_ffn):
    B, S, _ = x_prompt.shape
    DB, TS, _ = x_sample.shape
    depth = w_in.shape[0]
    n_pool, page = cache_k.shape[1], cache_k.shape[2]
    n_pages = page_table.shape[1]
    past = n_pages * page
    assert TS == SUBLANES and S % MOBA_BLOCK == 0 and page == LANES
    lp = -(-(past + TS) // MOBA_BLOCK) * MOBA_BLOCK

    cos_p, sin_p = _rope_tables(jnp.arange(S, dtype=jnp.int32))
    cos_s, sin_s = _rope_tables(jnp.tile(past + jnp.arange(TS, dtype=jnp.int32), DB))
    tri, ust = _tri_consts()
    ltri_incl = jnp.transpose(tri)
    ltri_p = _row_cumsum_matrix(PROJ_TM, S)
    ltri_s = _row_cumsum_matrix(PROJ_TM, TS)
    pr = np.arange(4 * page)
    head_prefix = jnp.asarray(((pr[:, None] // page == pr[None, :] // page)
                               & (pr[:, None] <= pr[None, :])).astype(np.float32), BF16)
    pgr = np.arange(n_pages)
    page_strict = jnp.asarray((pgr[None, :] < pgr[:, None]).astype(np.float32), BF16)

    k_pool = cache_k.transpose(0, 1, 3, 4, 2).reshape(depth * n_pool, MIX_WIDTH, page)
    v_pool = cache_v.transpose(0, 1, 3, 4, 2).reshape(depth * n_pool, MIX_WIDTH, page)
    kidx_pool = cache_kidx.transpose(0, 1, 3, 2).reshape(depth * n_pool, IDX_DIM, page)
    logf_pool = cache_logf.astype(F32).transpose(0, 1, 3, 2).reshape(depth * n_pool, 4 * page)

    yp = x_prompt.reshape(B * S, D_MODEL)
    ys = x_sample.reshape(DB * TS, D_MODEL)
    row2 = lambda v: v.reshape(1, -1)
    outs_p = [[] for _ in range(5)]
    outs_s = [[] for _ in range(5)]
    for l in range(depth):
        w = w_in[l]
        o0 = 3 * MIX_WIDTH
        o1 = o0 + IDX_HEADS * IDX_DIM
        o2 = o1 + IDX_DIM
        o3 = o2 + IDX_HEADS
        o4 = o3 + 4
        wqkv = w[:, :o0].astype(BF16)
        wqi = w[:, o0:o1].astype(BF16)
        wsm = jnp.concatenate([w[:, o1:o2], w[:, o1:o2], w[:, o2:o3], w[:, o3:o4], w[:, o3:o4],
                               jnp.zeros((D_MODEL, LANES - 12), F32)], axis=1).astype(BF16)
        wg = w[:, o4:].astype(BF16)
        wbr = w_branch[l].astype(BF16)
        wout = w_out[l].astype(BF16)
        wa = w_ffn_in[l][:, :D_FF].astype(BF16)
        wb = w_ffn_in[l][:, D_FF:].astype(BF16)
        wo = w_ffn_out[l].astype(BF16)
        cw = jnp.concatenate([conv_w[l], jnp.zeros((SUBLANES - CONV_WIDTH, D_FF), F32)], axis=0)
        cb = row2(conv_b[l])
        gk = row2(jnp.concatenate([g_kidx[l], g_kidx[l]]))
        bfp = row2(jnp.concatenate([jnp.zeros((4,), F32), b_forget[l], b_forget[l], jnp.zeros((LANES - 12,), F32)]))
        gpre, gpost = row2(g_pre_mix[l]), row2(g_post_mix[l])
        gpre_f, gpost_f = row2(g_pre_ffn[l]), row2(g_post_ffn[l])

        q, kf, vf, kb, vb, qi, kif, kib, misc, kbar = _proj_call(
            yp, gpre, wqkv, wqi, wsm, cos_p, sin_p, gk, bfp, ltri_p, seq_len=S, q_dtype=BF16)
        ckt = jnp.pad(misc[:, 8:12].reshape(B, S, 4).transpose(0, 2, 1), ((0, 0), (0, SUBLANES - 4), (0, 0)))
        ck4 = ckt.reshape(B, SUBLANES, S // ATTN_CK, ATTN_CK).transpose(0, 2, 1, 3)
        kbar_p = jnp.pad(kbar.reshape(B, S // MOBA_BLOCK, MIX_WIDTH).astype(BF16),
                         ((0, 0), (0, LANES - S // MOBA_BLOCK), (0, 0)))
        wit = jnp.pad(misc[:, 0:4].reshape(B, S, 4).transpose(0, 2, 1), ((0, 0), (0, SUBLANES - 4), (0, 0)))
        o = _pattn_call(q, qi, misc, wit, kb, vb, kib, ck4, kbar_p, ltri_incl, ust, B=B, S=S)
        yp = _merge_call(yp, o, gpre, wg, wbr, wout, gpost)
        yp, alast = _ffn_call(yp, gpre_f, wa, wb, wo, cw, cb, gpost_f, seq_len=S)
        outs_p[0].append(kf.reshape(B, S, N_HEADS, HEAD_DIM))
        outs_p[1].append(vf.reshape(B, S, N_HEADS, HEAD_DIM))
        outs_p[2].append(kif[:, :IDX_DIM].reshape(B, S, IDX_DIM))
        outs_p[3].append(misc[:, 4:8].reshape(B, S, 4))
        alast = alast.reshape(B, -1, SUBLANES, D_FF)[:, -1]
        outs_p[4].append(alast[:, SUBLANES - (CONV_WIDTH - 1):, :])

        q, kf, vf, kb, vb, qi, kif, kib, misc, kbar = _proj_call(
            ys, gpre, wqkv, wqi, wsm, cos_s, sin_s, gk, bfp, ltri_s, seq_len=TS, q_dtype=F32)
        sel, ckp, cqb, ckn = _sidx_call(page_table, qi, misc, kif, tri, head_prefix, page_strict,
                                        kidx_pool, logf_pool, lp=lp, pool_base=l * n_pool)
        o = _sattn_call(page_table, q, kf, vf, sel, ckp, cqb, ckn, ust, k_pool, v_pool,
                        lp=lp, pool_base=l * n_pool)
        ys = _merge_call(ys, o, gpre, wg, wbr, wout, gpost)
        st = state_conv[l].astype(F32)
        zeros = jnp.zeros((DB, TS - 2, D_FF), F32)
        e1 = jnp.concatenate([st[:, 1:2], jnp.zeros((DB, TS - 1, D_FF), F32)], axis=1).reshape(DB * TS, D_FF)
        e2 = jnp.concatenate([st, zeros], axis=1).reshape(DB * TS, D_FF)
        ys, a_all = _ffn_call(ys, gpre_f, wa, wb, wo, cw, cb, gpost_f, seq_len=TS, state=(e1, e2))
        outs_s[0].append(kf.reshape(DB, TS, N_HEADS, HEAD_DIM))
        outs_s[1].append(vf.reshape(DB, TS, N_HEADS, HEAD_DIM))
        outs_s[2].append(kif[:, :IDX_DIM].reshape(DB, TS, IDX_DIM))
        outs_s[3].append(misc[:, 4:8].reshape(DB, TS, 4))
        outs_s[4].append(a_all.reshape(DB, TS, D_FF)[:, TS - (CONV_WIDTH - 1):, :])

    return (yp.reshape(B, S, D_MODEL), ys.reshape(DB, TS, D_MODEL),
            *[jnp.stack(v) for v in outs_p], *[jnp.stack(v) for v in outs_s])
```
